```python
import math
import jax, jax.numpy as jnp
from jax import lax
import numpy as np

D_MODEL = 1024
BATCH = 8
SEQ = 2048
DEPTH = 4
DEC_BATCH = 128
DEC_SEQ = 4
PAST_LEN = 2048
PAGE_SIZE = 128

SSD_EXPAND = 2
D_INNER = SSD_EXPAND * D_MODEL
SSD_HEAD_DIM = 64
SSD_HEADS = D_INNER // SSD_HEAD_DIM
SSD_GROUPS = 8
SSD_HPG = SSD_HEADS // SSD_GROUPS
D_STATE = 128
CONV_WIDTH = 4
CONV_DIM = D_INNER + 2 * SSD_GROUPS * D_STATE
SSD_CHUNK = 128
ATTN_HEAD_DIM = 64
HEADS_PER_GROUP = 4
WINDOWS = (128, 512, 2048)
DILATIONS = (1, 4, 16)
N_DIL_GROUPS = 3
ATTN_HEADS = N_DIL_GROUPS * HEADS_PER_GROUP
ATTN_WIDTH = ATTN_HEADS * ATTN_HEAD_DIM
ATTN_OUT_WIDTH = HEADS_PER_GROUP * ATTN_HEAD_DIM
ATTN_BLOCK = 128
ATTN_SCALE = ATTN_HEAD_DIM ** -0.5
REL_BUCKETS = 32
REL_MAX_DIST = 2048
D_FF = 2816
FFN_RES = 0.5
NORM_EPS = 1e-6
IN_SIZES = (D_INNER, CONV_DIM, SSD_HEADS, ATTN_WIDTH, ATTN_WIDTH, ATTN_WIDTH, D_MODEL, D_MODEL)
IN_WIDTH = D_INNER + CONV_DIM + SSD_HEADS + 3 * ATTN_WIDTH + 2 * D_MODEL

kernel_name = 'hybrid_ssd_dilated_attn_decoder'


def split_cols(t, sizes):
    out, start = [], 0
    for s in sizes:
        out.append(t[..., start:start + s])
        start += s
    return out


def rmsnorm(x, g):
    xf = x.astype(jnp.float32)
    y = xf * lax.rsqrt(jnp.mean(xf * xf, axis=-1, keepdims=True) + NORM_EPS)
    return (y * g.astype(jnp.float32)).astype(x.dtype)


def grouped_rmsnorm(x, g, n_groups):
    shp = x.shape
    xf = x.astype(jnp.float32).reshape(shp[:-1] + (n_groups, shp[-1] // n_groups))
    y = xf * lax.rsqrt(jnp.mean(xf * xf, axis=-1, keepdims=True) + NORM_EPS)
    return (y.reshape(shp) * g.astype(jnp.float32)).astype(x.dtype)


def swiglu(x, w13, w2):
    a, b = jnp.split(x @ w13, 2, axis=-1)
    return (jax.nn.silu(a) * b) @ w2


def rel_bucket(dist):
    max_exact = REL_BUCKETS // 2
    d = jnp.maximum(dist, 1).astype(jnp.float32)
    large = max_exact + (jnp.log(d / max_exact) / math.log(REL_MAX_DIST / max_exact)
                         * (REL_BUCKETS - max_exact)).astype(jnp.int32)
    large = jnp.minimum(large, REL_BUCKETS - 1)
    return jnp.where(dist < max_exact, dist, large)


def dilated_attn_prompt(q, k, v, bias_tab, dil, window):
    b, s, h, e = q.shape
    n_sub = s // dil
    reach = window // dil
    blk = ATTN_BLOCK
    nb = -(-n_sub // blk)
    pad_end = nb * blk - n_sub

    def fold(t):
        return t.reshape(b, n_sub, dil, h, e).transpose(0, 2, 1, 3, 4).reshape(b * dil, n_sub, h, e)

    qf = jnp.pad(fold(q), ((0, 0), (0, pad_end), (0, 0), (0, 0)))
    kf = jnp.pad(fold(k), ((0, 0), (blk, pad_end), (0, 0), (0, 0)))
    vf = jnp.pad(fold(v), ((0, 0), (blk, pad_end), (0, 0), (0, 0)))
    qb = qf.reshape(b * dil, nb, blk, h, e)
    kb = kf.reshape(b * dil, nb + 1, blk, h, e)
    vb = vf.reshape(b * dil, nb + 1, blk, h, e)
    kb = jnp.concatenate([kb[:, :-1], kb[:, 1:]], axis=2)
    vb = jnp.concatenate([vb[:, :-1], vb[:, 1:]], axis=2)
    qi = jnp.arange(blk)[:, None] + blk
    ki = jnp.arange(2 * blk)[None, :]
    step = qi - ki
    blk_start = jnp.arange(nb)[:, None, None] * blk
    valid = (step >= 0) & (step <= reach) & (blk_start + ki[None] - blk >= 0)
    bias = bias_tab[rel_bucket(jnp.maximum(step, 0) * dil)].transpose(2, 0, 1)
    logits = jnp.einsum('znqhe,znkhe->znhqk', qb, kb).astype(jnp.float32) * ATTN_SCALE
    logits = logits + bias.astype(jnp.float32)[None, None]
    logits = jnp.where(valid[None, :, None], logits, -jnp.inf)
    lse = jax.nn.logsumexp(logits, axis=-1)
    p = jnp.exp(logits - lse[..., None])
    o = jnp.einsum('znhqk,znkhe->znqhe', p.astype(vb.dtype), vb)
    o = o.reshape(b * dil, nb * blk, h, e)[:, :n_sub]
    o = o.reshape(b, dil, n_sub, h, e).transpose(0, 2, 1, 3, 4).reshape(b, s, h, e)
    lse = lse.transpose(0, 1, 3, 2).reshape(b * dil, nb * blk, h)[:, :n_sub]
    lse = lse.reshape(b, dil, n_sub, h).transpose(0, 2, 1, 3).reshape(b, s, h)
    return o, lse


def dilated_attn_sample(q, k, v, kv_buf, bias_tab, dil, window):
    t = q.shape[1]
    lb = kv_buf.shape[1]
    k_all = jnp.concatenate([kv_buf[:, :, 0].astype(k.dtype), k], axis=1)
    v_all = jnp.concatenate([kv_buf[:, :, 1].astype(v.dtype), v], axis=1)
    n_keys = window // dil + 1
    j = jnp.arange(n_keys)
    idx = lb + jnp.arange(t)[:, None] - j[None, :] * dil
    valid = idx >= 0
    idx_c = jnp.maximum(idx, 0)
    kg = k_all[:, idx_c]
    vg = v_all[:, idx_c]
    bias = bias_tab[rel_bucket(j * dil)].T
    logits = jnp.einsum('bthe,btjhe->bthj', q, kg).astype(jnp.float32) * ATTN_SCALE
    logits = logits + bias.astype(jnp.float32)[None, None]
    logits = jnp.where(valid[None, :, None, :], logits, -jnp.inf)
    lse = jax.nn.logsumexp(logits, axis=-1)
    p = jnp.exp(logits - lse[..., None])
    o = jnp.einsum('bthj,btjhe->bthe', p.astype(vg.dtype), vg)
    return o, lse


def ssd_scan(x, dt, a_head, bmat, cmat, h0):
    bsz, seqlen = x.shape[:2]
    cl = min(SSD_CHUNK, seqlen)
    nc = -(-seqlen // cl)
    pad = nc * cl - seqlen

    def chunk(t):
        t = jnp.pad(t, [(0, 0), (0, pad)] + [(0, 0)] * (t.ndim - 2))
        return t.reshape((bsz, nc, cl) + t.shape[2:])

    xdt = chunk(x.astype(jnp.float32) * dt[..., None])
    a = chunk(dt * a_head)
    bc = chunk(bmat)
    cc = chunk(cmat)
    a_cs = jnp.cumsum(a, axis=2)
    causal = jnp.tril(jnp.ones((cl, cl), dtype=bool))
    seg = a_cs[:, :, :, None] - a_cs[:, :, None, :]
    decay = jnp.exp(jnp.where(causal[:, :, None, None], seg, -jnp.inf))
    cb = jnp.einsum('bclgn,bcsgn->bclsg', cc, bc).astype(jnp.float32)
    y_diag = jnp.einsum('bclsgr,bcsgrp->bclgrp', cb[..., None] * decay, xdt)
    to_end = jnp.exp(a_cs[:, :, -1:] - a_cs)
    states = jnp.einsum('bclgn,bclgrp->bcgrpn', bc.astype(jnp.float32), xdt * to_end[..., None])
    chunk_decay = jnp.exp(a_cs[:, :, -1])

    def step(h, inp):
        s_c, d_c = inp
        return h * d_c[..., None, None] + s_c, h

    h_last, h_prev = lax.scan(step, h0.astype(jnp.float32),
                              (jnp.moveaxis(states, 1, 0), jnp.moveaxis(chunk_decay, 1, 0)))
    h_prev = jnp.moveaxis(h_prev, 0, 1)
    y_off = jnp.einsum('bclgn,bcgrpn->bclgrp', cc.astype(jnp.float32), h_prev) * jnp.exp(a_cs)[..., None]
    y = (y_diag + y_off).reshape((bsz, nc * cl) + x.shape[2:])[:, :seqlen]
    return y.astype(x.dtype), h_last.astype(h0.dtype)


def token_mixing(u, conv_state, ssm_state, kv_bufs, rel_bias, w_in, conv_w, conv_b, dt_bias, a_log,
                 d_skip, ssd_norm, w_ssd_out, w_attn_out, w_o):
    bsz, seqlen, _ = u.shape
    z, xbc, dt_raw, q, k, v, g_ssd, g_attn = split_cols(u @ w_in, IN_SIZES)
    xbc_full = jnp.concatenate([conv_state.astype(xbc.dtype), xbc], axis=1)
    new_conv = xbc_full[:, -(CONV_WIDTH - 1):]
    conv = conv_b + sum(xbc_full[:, i:i + seqlen] * conv_w[i] for i in range(CONV_WIDTH))
    conv = jax.nn.silu(conv)
    xs, bmat, cmat = split_cols(conv, (D_INNER, SSD_GROUPS * D_STATE, SSD_GROUPS * D_STATE))
    dt = jax.nn.softplus(dt_raw.astype(jnp.float32) + dt_bias.astype(jnp.float32))
    a_head = -jnp.exp(a_log.astype(jnp.float32))
    xs5 = xs.reshape(bsz, seqlen, SSD_GROUPS, SSD_HPG, SSD_HEAD_DIM)
    y, new_ssm = ssd_scan(xs5, dt.reshape(bsz, seqlen, SSD_GROUPS, SSD_HPG),
                          a_head.reshape(SSD_GROUPS, SSD_HPG),
                          bmat.reshape(bsz, seqlen, SSD_GROUPS, D_STATE),
                          cmat.reshape(bsz, seqlen, SSD_GROUPS, D_STATE),
                          ssm_state.reshape(bsz, SSD_GROUPS, SSD_HPG, SSD_HEAD_DIM, D_STATE))
    y = y + d_skip.reshape(SSD_GROUPS, SSD_HPG)[:, :, None].astype(y.dtype) * xs5
    y = grouped_rmsnorm(y.reshape(bsz, seqlen, D_INNER) * jax.nn.silu(z), ssd_norm, SSD_GROUPS)
    ssd_out = y @ w_ssd_out
    q = q.reshape(bsz, seqlen, ATTN_HEADS, ATTN_HEAD_DIM)
    k = k.reshape(bsz, seqlen, ATTN_HEADS, ATTN_HEAD_DIM)
    v = v.reshape(bsz, seqlen, ATTN_HEADS, ATTN_HEAD_DIM)
    outs, lses, new_kv = [], [], []
    for gi in range(N_DIL_GROUPS):
        hsl = slice(gi * HEADS_PER_GROUP, (gi + 1) * HEADS_PER_GROUP)
        qg, kg, vg, bias_g = q[:, :, hsl], k[:, :, hsl], v[:, :, hsl], rel_bias[:, hsl]
        if kv_bufs is None:
            o, lse = dilated_attn_prompt(qg, kg, vg, bias_g, DILATIONS[gi], WINDOWS[gi])
            rows = jnp.stack([kg, vg], axis=2)[:, -min(WINDOWS[gi], seqlen):]
        else:
            o, lse = dilated_attn_sample(qg, kg, vg, kv_bufs[gi], bias_g, DILATIONS[gi], WINDOWS[gi])
            rows = jnp.stack([kg, vg], axis=2)
        outs.append(o)
        lses.append(lse)
        new_kv.append(rows)
    wts = jax.nn.softmax(jnp.stack(lses, axis=0), axis=0)
    o = jnp.sum(wts[..., None] * jnp.stack(outs, axis=0).astype(jnp.float32), axis=0)
    attn_out = o.astype(u.dtype).reshape(bsz, seqlen, ATTN_OUT_WIDTH) @ w_attn_out
    merged = jax.nn.sigmoid(g_ssd) * ssd_out + jax.nn.sigmoid(g_attn) * attn_out
    new_ssm = new_ssm.reshape(bsz, SSD_HEADS, SSD_HEAD_DIM, D_STATE)
    return merged @ w_o, new_conv, new_ssm, new_kv


def trunk_layer(x, conv_state, ssm_state, kv_bufs, rel_bias, lw):
    (f1n, f1w13, f1w2, mn, w_in, conv_w, conv_b, dt_bias, a_log, d_skip, ssd_norm,
     w_ssd_out, w_attn_out, w_o, f2n, f2w13, f2w2) = lw
    x = x + FFN_RES * swiglu(rmsnorm(x, f1n), f1w13, f1w2)
    mix, new_conv, new_ssm, new_kv = token_mixing(rmsnorm(x, mn), conv_state, ssm_state, kv_bufs, rel_bias,
                                                  w_in, conv_w, conv_b, dt_bias, a_log, d_skip, ssd_norm,
                                                  w_ssd_out, w_attn_out, w_o)
    x = x + mix
    x = x + FFN_RES * swiglu(rmsnorm(x, f2n), f2w13, f2w2)
    return x, new_conv, new_ssm, new_kv


def setup_inputs(seed: int = 0) -> dict:
    key = jax.random.key(seed)
    ks = iter(jax.random.split(key, 32))

    def nrm(shape, scale):
        return jax.random.normal(next(ks), shape, jnp.float32) * scale

    def gain(shape):
        return 1.0 + nrm(shape, 0.02)

    x_prompt = nrm((BATCH, SEQ, D_MODEL), 1.0)
    x_sample = nrm((DEC_BATCH, DEC_SEQ, D_MODEL), 1.0)
    kv_tail = (2, HEADS_PER_GROUP, ATTN_HEAD_DIM)
    cache_kv_w128 = nrm((DEPTH, DEC_BATCH, min(WINDOWS[0], PAST_LEN)) + kv_tail, 1.0)
    cache_kv_w512 = nrm((DEPTH, DEC_BATCH, min(WINDOWS[1], PAST_LEN)) + kv_tail, 1.0)
    cache_kv_w2048 = nrm((DEPTH, DEC_BATCH, min(WINDOWS[2], PAST_LEN)) + kv_tail, 1.0)
    state_conv = nrm((DEPTH, DEC_BATCH, CONV_WIDTH - 1, CONV_DIM), 1.0)
    state_ssm = nrm((DEPTH, DEC_BATCH, SSD_HEADS, SSD_HEAD_DIM, D_STATE), 0.1)
    rel_bias = nrm((REL_BUCKETS, ATTN_HEADS), 0.5)
    ffn1_norm = gain((DEPTH, D_MODEL))
    ffn1_w13 = nrm((DEPTH, D_MODEL, 2 * D_FF), D_MODEL ** -0.5)
    ffn1_w2 = nrm((DEPTH, D_FF, D_MODEL), D_FF ** -0.5)
    mix_norm = gain((DEPTH, D_MODEL))
    w_in = nrm((DEPTH, D_MODEL, IN_WIDTH), D_MODEL ** -0.5)
    conv_w = nrm((DEPTH, CONV_WIDTH, CONV_DIM), CONV_WIDTH ** -0.5)
    conv_b = nrm((DEPTH, CONV_DIM), 0.02)
    dt0 = jnp.exp(jax.random.uniform(next(ks), (DEPTH, SSD_HEADS), jnp.float32,
                                     math.log(1e-3), math.log(1e-1)))
    dt_bias = dt0 + jnp.log(-jnp.expm1(-dt0))
    a_log = jnp.log(jax.random.uniform(next(ks), (DEPTH, SSD_HEADS), jnp.float32, 1.0, 16.0))
    d_skip = 1.0 + nrm((DEPTH, SSD_HEADS), 0.02)
    ssd_norm = gain((DEPTH, D_INNER))
    w_ssd_out = nrm((DEPTH, D_INNER, D_MODEL), D_INNER ** -0.5)
    w_attn_out = nrm((DEPTH, ATTN_OUT_WIDTH, D_MODEL), ATTN_OUT_WIDTH ** -0.5)
    w_o = nrm((DEPTH, D_MODEL, D_MODEL), D_MODEL ** -0.5)
    ffn2_norm = gain((DEPTH, D_MODEL))
    ffn2_w13 = nrm((DEPTH, D_MODEL, 2 * D_FF), D_MODEL ** -0.5)
    ffn2_w2 = nrm((DEPTH, D_FF, D_MODEL), D_FF ** -0.5)
    final_norm = gain((D_MODEL,))
    return {'x_prompt': x_prompt, 'x_sample': x_sample,
            'cache_kv_w128': cache_kv_w128, 'cache_kv_w512': cache_kv_w512, 'cache_kv_w2048': cache_kv_w2048,
            'state_conv': state_conv, 'state_ssm': state_ssm, 'rel_bias': rel_bias,
            'ffn1_norm': ffn1_norm, 'ffn1_w13': ffn1_w13, 'ffn1_w2': ffn1_w2, 'mix_norm': mix_norm,
            'w_in': w_in, 'conv_w': conv_w, 'conv_b': conv_b, 'dt_bias': dt_bias, 'a_log': a_log,
            'd_skip': d_skip, 'ssd_norm': ssd_norm, 'w_ssd_out': w_ssd_out, 'w_attn_out': w_attn_out,
            'w_o': w_o, 'ffn2_norm': ffn2_norm, 'ffn2_w13': ffn2_w13, 'ffn2_w2': ffn2_w2,
            'final_norm': final_norm}


def reference(x_prompt, x_sample, cache_kv_w128, cache_kv_w512, cache_kv_w2048, state_conv, state_ssm,
              rel_bias, ffn1_norm, ffn1_w13, ffn1_w2, mix_norm, w_in, conv_w, conv_b, dt_bias, a_log,
              d_skip, ssd_norm, w_ssd_out, w_attn_out, w_o, ffn2_norm, ffn2_w13, ffn2_w2, final_norm):
    hp, hs = x_prompt, x_sample
    bp = x_prompt.shape[0]
    kv_p = [[], [], []]
    kv_s = [[], [], []]
    conv_p, conv_s, ssm_p, ssm_s = [], [], [], []
    for l in range(DEPTH):
        lw = (ffn1_norm[l], ffn1_w13[l], ffn1_w2[l], mix_norm[l], w_in[l], conv_w[l], conv_b[l],
              dt_bias[l], a_log[l], d_skip[l], ssd_norm[l], w_ssd_out[l], w_attn_out[l], w_o[l],
              ffn2_norm[l], ffn2_w13[l], ffn2_w2[l])
        zc = jnp.zeros((bp, CONV_WIDTH - 1, CONV_DIM), hp.dtype)
        zs = jnp.zeros((bp, SSD_HEADS, SSD_HEAD_DIM, D_STATE), hp.dtype)
        hp, c_p, s_p, kvn_p = trunk_layer(hp, zc, zs, None, rel_bias, lw)
        bufs = (cache_kv_w128[l], cache_kv_w512[l], cache_kv_w2048[l])
        hs, c_s, s_s, kvn_s = trunk_layer(hs, state_conv[l], state_ssm[l], bufs, rel_bias, lw)
        conv_p.append(c_p)
        ssm_p.append(s_p)
        conv_s.append(c_s)
        ssm_s.append(s_s)
        for gi in range(N_DIL_GROUPS):
            kv_p[gi].append(kvn_p[gi])
            kv_s[gi].append(kvn_s[gi])
    y_prompt = rmsnorm(hp, final_norm)
    y_sample = rmsnorm(hs, final_norm)
    return (y_prompt, y_sample,
            jnp.stack(kv_p[0]), jnp.stack(kv_p[1]), jnp.stack(kv_p[2]), jnp.stack(conv_p), jnp.stack(ssm_p),
            jnp.stack(kv_s[0]), jnp.stack(kv_s[1]), jnp.stack(kv_s[2]), jnp.stack(conv_s), jnp.stack(ssm_s))
```

```python
import functools
import math

import jax
import jax.numpy as jnp
from jax import lax
from jax.experimental import pallas as pl
from jax.experimental.pallas import tpu as pltpu

F32 = jnp.float32
BF16 = jnp.bfloat16

D_MODEL = 1024
DEPTH = 4
D_INNER = 2048
SSD_HEAD_DIM = 64
SSD_HEADS = 32
SSD_GROUPS = 8
SSD_HPG = 4
D_STATE = 128
CONV_WIDTH = 4
CONV_DIM = 4096
SSD_CHUNK = 128
ATTN_HEAD_DIM = 64
HEADS_PER_GROUP = 4
WINDOWS = (128, 512, 2048)
DILATIONS = (1, 4, 16)
N_DIL_GROUPS = 3
ATTN_WIDTH = 768
ATTN_GROUP_WIDTH = HEADS_PER_GROUP * ATTN_HEAD_DIM
ATTN_BLOCK = 128
ATTN_SCALE = ATTN_HEAD_DIM ** -0.5
REL_BUCKETS = 32
REL_MAX_DIST = 2048
D_FF = 2816
FFN_RES = 0.5
NORM_EPS = 1e-6

LANES = 128
SEG = 8
SEG_T0 = CONV_WIDTH - 1

COL_XBC = 0
COL_Z = COL_XBC + CONV_DIM
COL_GS = COL_Z + D_INNER
COL_GA = COL_GS + D_MODEL
COL_QKV = COL_GA + D_MODEL
COL_DT = COL_QKV + 3 * ATTN_WIDTH
PROJ_W = 10752
PROJ_TN = 1536

VMEM_LIMIT = 56 * 1024 * 1024


def _cparams(sem):
    return pltpu.CompilerParams(dimension_semantics=sem, vmem_limit_bytes=VMEM_LIMIT)


def _resident(shape):
    nd = len(shape)
    return pl.BlockSpec(shape, lambda *_: (0,) * nd, pipeline_mode=pl.Buffered(1))


def _rms(x, g):
    return x * lax.rsqrt(jnp.mean(x * x, axis=-1, keepdims=True) + NORM_EPS) * g


def _silu(x):
    return x * jax.nn.sigmoid(x)


def _dot(a, b):
    return jnp.dot(a, b, preferred_element_type=F32)


def _dot_nt(a, b):
    return lax.dot_general(a, b, (((1,), (1,)), ((), ())), preferred_element_type=F32)


def _dot_tn(a, b):
    return lax.dot_general(a, b, (((0,), (0,)), ((), ())), preferred_element_type=F32)


def _ffn_kernel(x_ref, g_ref, w13_ref, w2_ref, o_ref, *, n_chunks):
    x = x_ref[...]
    xn = _rms(x, g_ref[...]).astype(BF16)
    tf = D_FF // n_chunks
    acc = None
    for c in range(n_chunks):
        a = _dot(xn, w13_ref[:, c * tf:(c + 1) * tf])
        b = _dot(xn, w13_ref[:, D_FF + c * tf:D_FF + (c + 1) * tf])
        h = (_silu(a) * b).astype(BF16)
        d = _dot(h, w2_ref[c * tf:(c + 1) * tf, :])
        acc = d if acc is None else acc + d
    o_ref[...] = x + FFN_RES * acc


def _ffn(x, g, w13, w2, tm):
    n = x.shape[0]
    return pl.pallas_call(
        functools.partial(_ffn_kernel, n_chunks=2),
        grid=(n // tm,),
        in_specs=[pl.BlockSpec((tm, D_MODEL), lambda i: (i, 0)),
                  _resident((1, D_MODEL)),
                  _resident((D_MODEL, 2 * D_FF)),
                  _resident((D_FF, D_MODEL))],
        out_specs=pl.BlockSpec((tm, D_MODEL), lambda i: (i, 0)),
        out_shape=jax.ShapeDtypeStruct((n, D_MODEL), F32),
        compiler_params=_cparams(("parallel",)),
        name="ffn",
    )(x, g, w13, w2)


def _inproj_kernel(x_ref, g_ref, w_ref, o_ref, xn_ref):
    @pl.when(pl.program_id(1) == 0)
    def _():
        xn_ref[...] = _rms(x_ref[...], g_ref[...]).astype(BF16)

    o_ref[...] = _dot(xn_ref[...], w_ref[...])


def _inproj(x, g, w, tm):
    n = x.shape[0]
    return pl.pallas_call(
        _inproj_kernel,
        grid=(n // tm, PROJ_W // PROJ_TN),
        in_specs=[pl.BlockSpec((tm, D_MODEL), lambda i, j: (i, 0)),
                  pl.BlockSpec((1, D_MODEL), lambda i, j: (0, 0)),
                  pl.BlockSpec((D_MODEL, PROJ_TN), lambda i, j: (0, j))],
        out_specs=pl.BlockSpec((tm, PROJ_TN), lambda i, j: (i, j)),
        out_shape=jax.ShapeDtypeStruct((n, PROJ_W), F32),
        scratch_shapes=[pltpu.VMEM((tm, D_MODEL), BF16)],
        compiler_params=_cparams(("parallel", "arbitrary")),
        name="inproj",
    )(x, g, w)


def _ssd_kernel(*refs, rows, seg_len, has_state, token_lo, token_hi):
    if has_state:
        (xbc_ref, z_ref, dt_ref, cw_ref, cb_ref, dtb_ref, alog_ref, dskip_ref, norm_ref,
         csm_ref, totm_ref, mask_ref, cst_ref, st_in_ref,
         y_ref, st_ref, xpad_ref, conv_ref, yscr_ref) = refs
    else:
        (xbc_ref, z_ref, dt_ref, cw_ref, cb_ref, dtb_ref, alog_ref, dskip_ref, norm_ref,
         csm_ref, totm_ref, mask_ref,
         y_ref, st_ref, xpad_ref, conv_ref, yscr_ref) = refs
    nseg = rows // seg_len
    c = pl.program_id(1)

    @pl.when(c == 0)
    def _():
        xpad_ref[0:8, :] = jnp.zeros((8, CONV_DIM), F32)
        if has_state:
            st_ref[...] = st_in_ref[...]
        else:
            st_ref[...] = jnp.zeros(st_ref.shape, F32)

    row_in_seg = lax.broadcasted_iota(jnp.int32, (rows, 1), 0) % seg_len
    xbc = xbc_ref[...]
    if has_state:
        xbc = jnp.where(row_in_seg < SEG_T0, cst_ref[...], xbc)
    xpad_ref[8:8 + rows, :] = xbc
    conv = cb_ref[...] + xbc * cw_ref[CONV_WIDTH - 1:CONV_WIDTH, :]
    for i in range(CONV_WIDTH - 1):
        conv = conv + xpad_ref[5 + i:5 + i + rows, :] * cw_ref[i:i + 1, :]
    xpad_ref[0:8, :] = xbc[rows - 8:rows, :]
    conv_ref[...] = _silu(conv)

    is_token = (row_in_seg >= token_lo) & (row_in_seg < token_hi)
    dtr = dt_ref[...] + dtb_ref[...]
    dt = jnp.maximum(dtr, 0.0) + jnp.log1p(jnp.exp(-jnp.abs(dtr)))
    dt = jnp.where(is_token, dt, 0.0)
    a = dt * (-jnp.exp(alog_ref[...]))
    a_cs = jnp.dot(csm_ref[...], a, preferred_element_type=F32, precision=lax.Precision.HIGHEST)
    a_tot = jnp.dot(totm_ref[...], a, preferred_element_type=F32, precision=lax.Precision.HIGHEST)
    if rows < LANES:
        a_cs_sq = jnp.concatenate([a_cs, jnp.zeros((LANES - rows, LANES), F32)], axis=0)
    else:
        a_cs_sq = a_cs
    a_cs_t = a_cs_sq.T
    to_end = jnp.exp(a_tot - a_cs)
    ea = jnp.exp(a_cs)
    cdec = jnp.exp(a_tot)
    mask = mask_ref[...] > 0.5

    for g in range(SSD_GROUPS):
        bg = conv_ref[:, D_INNER + g * D_STATE:D_INNER + (g + 1) * D_STATE].astype(BF16)
        cg = conv_ref[:, D_INNER + SSD_GROUPS * D_STATE + g * D_STATE:
                      D_INNER + SSD_GROUPS * D_STATE + (g + 1) * D_STATE].astype(BF16)
        cbm = _dot_nt(cg, bg)
        w_parts = []
        for r in range(SSD_HPG):
            h = g * SSD_HPG + r
            seg = a_cs[:, h:h + 1] - a_cs_t[h:h + 1, 0:rows]
            decay = jnp.exp(jnp.where(mask, seg, -jnp.inf))
            m_h = (cbm * decay).astype(BF16)
            xdt = conv_ref[:, h * SSD_HEAD_DIM:(h + 1) * SSD_HEAD_DIM] * dt[:, h:h + 1]
            yscr_ref[:, h * SSD_HEAD_DIM:(h + 1) * SSD_HEAD_DIM] = _dot(m_h, xdt.astype(BF16))
            w_parts.append((xdt * to_end[:, h:h + 1]).astype(BF16))
        w_g = jnp.concatenate(w_parts, axis=1)
        for j in range(nseg):
            r0 = j * seg_len
            s_old = st_ref[j, g * SSD_HPG:(g + 1) * SSD_HPG].reshape(SSD_HPG * SSD_HEAD_DIM, D_STATE)
            yo = _dot_nt(cg[r0:r0 + seg_len], s_old.astype(BF16))
            dec_parts = []
            for r in range(SSD_HPG):
                h = g * SSD_HPG + r
                lo = h * SSD_HEAD_DIM
                yscr_ref[r0:r0 + seg_len, lo:lo + SSD_HEAD_DIM] += (
                    yo[:, r * SSD_HEAD_DIM:(r + 1) * SSD_HEAD_DIM] * ea[r0:r0 + seg_len, h:h + 1])
                dec_parts.append(jnp.broadcast_to(cdec[r0:r0 + 1, h:h + 1], (SSD_HEAD_DIM, D_STATE)))
            dec = jnp.concatenate(dec_parts, axis=0)
            s_new = s_old * dec + _dot_tn(w_g[r0:r0 + seg_len], bg[r0:r0 + seg_len])
            st_ref[j, g * SSD_HPG:(g + 1) * SSD_HPG] = s_new.reshape(SSD_HPG, SSD_HEAD_DIM, D_STATE)

    gw = D_INNER // SSD_GROUPS
    for g in range(SSD_GROUPS):
        sl = slice(g * gw, (g + 1) * gw)
        y = yscr_ref[:, sl] + dskip_ref[:, sl] * conv_ref[:, sl]
        y = y * _silu(z_ref[:, sl])
        y = y * lax.rsqrt(jnp.mean(y * y, axis=-1, keepdims=True) + NORM_EPS) * norm_ref[:, sl]
        y_ref[:, sl] = y.astype(BF16)


def _ssd(proj, lw, mats, n_outer, n_inner, rows, seg_len, token_lo, token_hi, conv_state=None, ssm_state=None):
    has_state = ssm_state is not None
    nseg = rows // seg_len
    n = proj.shape[0]

    def rowmap(cb):
        return lambda o, c: (o * n_inner + c, cb)

    def const2(shape):
        return pl.BlockSpec(shape, lambda o, c: (0, 0))

    in_specs = [pl.BlockSpec((rows, CONV_DIM), rowmap(COL_XBC // CONV_DIM)),
                pl.BlockSpec((rows, D_INNER), rowmap(COL_Z // D_INNER)),
                pl.BlockSpec((rows, LANES), rowmap(COL_DT // LANES)),
                const2((CONV_WIDTH, CONV_DIM)), const2((1, CONV_DIM)),
                const2((1, LANES)), const2((1, LANES)),
                const2((1, D_INNER)), const2((1, D_INNER)),
                const2((rows, rows)), const2((rows, rows)), const2((rows, rows))]
    args = [proj, proj, proj, lw["conv_w"], lw["conv_b"], lw["dt_bias"], lw["a_log"],
            lw["d_skip"], lw["ssd_norm"], mats[0], mats[1], mats[2]]
    if has_state:
        in_specs += [pl.BlockSpec((rows, CONV_DIM), rowmap(0)),
                     pl.BlockSpec((nseg, SSD_HEADS, SSD_HEAD_DIM, D_STATE), lambda o, c: (o, 0, 0, 0))]
        args += [conv_state, ssm_state]
    y, st = pl.pallas_call(
        functools.partial(_ssd_kernel, rows=rows, seg_len=seg_len, has_state=has_state,
                          token_lo=token_lo, token_hi=token_hi),
        grid=(n_outer, n_inner),
        in_specs=in_specs,
        out_specs=[pl.BlockSpec((rows, D_INNER), rowmap(0)),
                   pl.BlockSpec((nseg, SSD_HEADS, SSD_HEAD_DIM, D_STATE), lambda o, c: (o, 0, 0, 0))],
        out_shape=[jax.ShapeDtypeStruct((n, D_INNER), BF16),
                   jax.ShapeDtypeStruct((n_outer * nseg, SSD_HEADS, SSD_HEAD_DIM, D_STATE), F32)],
        scratch_shapes=[pltpu.VMEM((8 + rows, CONV_DIM), F32),
                        pltpu.VMEM((rows, CONV_DIM), F32),
                        pltpu.VMEM((rows, D_INNER), F32)],
        compiler_params=_cparams(("parallel", "arbitrary")),
        name="ssd_state" if has_state else "ssd_prompt",
    )(*args)
    return y, st


def _attn_prompt_kernel(q0, q1, k0, k1, v0, v1, bias_ref, o_ref, lse_ref, *, dil, seq):
    nblk = seq // (ATTN_BLOCK * dil)
    lo_lanes = lax.broadcasted_iota(jnp.int32, (ATTN_BLOCK, LANES), 1) < ATTN_HEAD_DIM

    def rows(start):
        if dil == 1:
            return pl.ds(pl.multiple_of(start, ATTN_BLOCK), ATTN_BLOCK)
        return pl.ds(start, ATTN_BLOCK, stride=dil)

    def body(i, carry):
        r = i // nblk
        blk = i % nblk
        cur = rows(r + blk * ATTN_BLOCK * dil)
        prev = rows(r + jnp.maximum(blk - 1, 0) * ATTN_BLOCK * dil)
        tab = jnp.minimum(blk, 1)
        for half, (q, k, v) in enumerate(((q0, k0, v0), (q1, k1, v1))):
            qb = q[cur, :]
            kk = jnp.concatenate([k[prev, :], k[cur, :]], axis=0).astype(BF16)
            vv = jnp.concatenate([v[prev, :], v[cur, :]], axis=0).astype(BF16)
            outs, lses = [], []
            for hh in range(2):
                lanes = lo_lanes if hh == 0 else jnp.logical_not(lo_lanes)
                qm = jnp.where(lanes, qb, 0.0).astype(BF16)
                s = _dot_nt(qm, kk) * ATTN_SCALE + bias_ref[tab, 2 * half + hh]
                m = jnp.max(s, axis=-1, keepdims=True)
                p = jnp.exp(s - m)
                l = jnp.sum(p, axis=-1, keepdims=True)
                outs.append(_dot(p.astype(BF16), vv) / l)
                lses.append(jnp.broadcast_to(m + jnp.log(l), (ATTN_BLOCK, LANES)))
            o_ref[half, cur, :] = jnp.where(lo_lanes, outs[0], outs[1])
            lse_ref[half, cur, :] = jnp.where(lo_lanes, lses[0], lses[1])
        return carry

    lax.fori_loop(0, dil * nblk, body, 0)


def _attn_prompt(proj, bias, gi, batch, seq):
    dil = DILATIONS[gi]
    base = (COL_QKV + gi * ATTN_WIDTH) // LANES

    def col(j):
        return pl.BlockSpec((seq, LANES), lambda b: (b, base + j))

    out_spec = pl.BlockSpec((2, seq, LANES), lambda b: (0, b, 0))
    shape = jax.ShapeDtypeStruct((2, batch * seq, LANES), F32)
    return pl.pallas_call(
        functools.partial(_attn_prompt_kernel, dil=dil, seq=seq),
        grid=(batch,),
        in_specs=[col(0), col(1), col(2), col(3), col(4), col(5),
                  pl.BlockSpec((2, HEADS_PER_GROUP, ATTN_BLOCK, 2 * ATTN_BLOCK), lambda b: (0, 0, 0, 0))],
        out_specs=[out_spec, out_spec],
        out_shape=[shape, shape],
        compiler_params=_cparams(("parallel",)),
        name=f"attn_prompt_d{dil}",
    )(proj, proj, proj, proj, proj, proj, bias)


def _attn_sample_kernel(q0, q1, kn0, kn1, vn0, vn1, cache_ref, bias_c_ref, bias_n_ref, o_ref, lse_ref, *, bt):
    lo_lanes = lax.broadcasted_iota(jnp.int32, (SEG, LANES), 1) < ATTN_HEAD_DIM

    def body(b, carry):
        seg = pl.ds(pl.multiple_of(b * SEG, SEG), SEG)
        for half, (q, kn, vn) in enumerate(((q0, kn0, vn0), (q1, kn1, vn1))):
            qb = q[seg, :]
            knew = kn[seg, :].astype(BF16)
            vnew = vn[seg, :].astype(BF16)
            kc = cache_ref[b, :, half * LANES:(half + 1) * LANES].astype(BF16)
            vc = cache_ref[b, :, ATTN_GROUP_WIDTH + half * LANES:
                           ATTN_GROUP_WIDTH + (half + 1) * LANES].astype(BF16)
            outs, lses = [], []
            for hh in range(2):
                lanes = lo_lanes if hh == 0 else jnp.logical_not(lo_lanes)
                qm = jnp.where(lanes, qb, 0.0).astype(BF16)
                sc = _dot_nt(qm, kc) * ATTN_SCALE + bias_c_ref[2 * half + hh]
                sn = _dot_nt(qm, knew) * ATTN_SCALE + bias_n_ref[2 * half + hh]
                m = jnp.maximum(jnp.max(sc, axis=-1, keepdims=True), jnp.max(sn, axis=-1, keepdims=True))
                pc = jnp.exp(sc - m)
                pn = jnp.exp(sn - m)
                l = jnp.sum(pc, axis=-1, keepdims=True) + jnp.sum(pn, axis=-1, keepdims=True)
                o = _dot(pc.astype(BF16), vc) + _dot(pn.astype(BF16), vnew)
                outs.append(o / l)
                lses.append(jnp.broadcast_to(m + jnp.log(l), (SEG, LANES)))
            o_ref[half, seg, :] = jnp.where(lo_lanes, outs[0], outs[1])
            lse_ref[half, seg, :] = jnp.where(lo_lanes, lses[0], lses[1])
        return carry

    lax.fori_loop(0, bt, body, 0)


def _attn_sample(proj, cache, bias_c, bias_n, gi, bt):
    nb = cache.shape[0]
    rows_c = cache.shape[1]
    base = (COL_QKV + gi * ATTN_WIDTH) // LANES

    def col(j):
        return pl.BlockSpec((bt * SEG, LANES), lambda i: (i, base + j))

    out_spec = pl.BlockSpec((2, bt * SEG, LANES), lambda i: (0, i, 0))
    shape = jax.ShapeDtypeStruct((2, nb * SEG, LANES), F32)
    return pl.pallas_call(
        functools.partial(_attn_sample_kernel, bt=bt),
        grid=(nb // bt,),
        in_specs=[col(0), col(1), col(2), col(3), col(4), col(5),
                  pl.BlockSpec((bt, rows_c, 2 * ATTN_GROUP_WIDTH), lambda i: (i, 0, 0)),
                  pl.BlockSpec((HEADS_PER_GROUP, SEG, rows_c), lambda i: (0, 0, 0)),
                  pl.BlockSpec((HEADS_PER_GROUP, SEG, SEG), lambda i: (0, 0, 0))],
        out_specs=[out_spec, out_spec],
        out_shape=[shape, shape],
        compiler_params=_cparams(("parallel",)),
        name=f"attn_sample_d{DILATIONS[gi]}",
    )(proj, proj, proj, proj, proj, proj, cache, bias_c, bias_n)


def _mix_out_kernel(x_ref, y_ref, gs_ref, ga_ref, o0, o1, o2, l0, l1, l2, wso_ref, wao_ref, wo_ref, out_ref,
                    *, token_lo, token_hi, seg_len):
    ssd_out = _dot(y_ref[...], wso_ref[...])
    attn_out = None
    for half in range(2):
        ls = [l0[half], l1[half], l2[half]]
        os_ = [o0[half], o1[half], o2[half]]
        m = jnp.maximum(jnp.maximum(ls[0], ls[1]), ls[2])
        es = [jnp.exp(l - m) for l in ls]
        den = es[0] + es[1] + es[2]
        o = (es[0] / den) * os_[0] + (es[1] / den) * os_[1] + (es[2] / den) * os_[2]
        d = _dot(o.astype(BF16), wao_ref[half * LANES:(half + 1) * LANES, :])
        attn_out = d if attn_out is None else attn_out + d
    merged = jax.nn.sigmoid(gs_ref[...]) * ssd_out + jax.nn.sigmoid(ga_ref[...]) * attn_out
    upd = _dot(merged.astype(BF16), wo_ref[...])
    if seg_len is not None:
        rows = x_ref.shape[0]
        ris = lax.broadcasted_iota(jnp.int32, (rows, 1), 0) % seg_len
        upd = jnp.where((ris >= token_lo) & (ris < token_hi), upd, 0.0)
    out_ref[...] = x_ref[...] + upd


def _mix_out(x, y, proj, attn, lw, tm, seg_len=None):
    n = x.shape[0]
    half_spec = pl.BlockSpec((2, tm, LANES), lambda i: (0, i, 0))
    (o0, l0), (o1, l1), (o2, l2) = attn
    return pl.pallas_call(
        functools.partial(_mix_out_kernel, token_lo=SEG_T0, token_hi=SEG_T0 + 4, seg_len=seg_len),
        grid=(n // tm,),
        in_specs=[pl.BlockSpec((tm, D_MODEL), lambda i: (i, 0)),
                  pl.BlockSpec((tm, D_INNER), lambda i: (i, 0)),
                  pl.BlockSpec((tm, D_MODEL), lambda i: (i, COL_GS // D_MODEL)),
                  pl.BlockSpec((tm, D_MODEL), lambda i: (i, COL_GA // D_MODEL)),
                  half_spec, half_spec, half_spec, half_spec, half_spec, half_spec,
                  _resident((D_INNER, D_MODEL)),
                  _resident((ATTN_GROUP_WIDTH, D_MODEL)),
                  _resident((D_MODEL, D_MODEL))],
        out_specs=pl.BlockSpec((tm, D_MODEL), lambda i: (i, 0)),
        out_shape=jax.ShapeDtypeStruct((n, D_MODEL), F32),
        compiler_params=_cparams(("parallel",)),
        name="mix_out",
    )(x, y, proj, proj, o0, o1, o2, l0, l1, l2, lw["w_ssd_out"], lw["w_attn_out"], lw["w_o"])


def _final_norm_kernel(x_ref, g_ref, o_ref):
    o_ref[...] = _rms(x_ref[...], g_ref[...])


def _final_norm(x, g, tm):
    n = x.shape[0]
    return pl.pallas_call(
        _final_norm_kernel,
        grid=(n // tm,),
        in_specs=[pl.BlockSpec((tm, D_MODEL), lambda i: (i, 0)),
                  pl.BlockSpec((1, D_MODEL), lambda i: (0, 0))],
        out_specs=pl.BlockSpec((tm, D_MODEL), lambda i: (i, 0)),
        out_shape=jax.ShapeDtypeStruct((n, D_MODEL), F32),
        compiler_params=_cparams(("parallel",)),
        name="final_norm",
    )(x, g)


def _rel_bucket(dist):
    max_exact = REL_BUCKETS // 2
    d = jnp.maximum(dist, 1).astype(F32)
    large = max_exact + (jnp.log(d / max_exact) / math.log(REL_MAX_DIST / max_exact)
                         * (REL_BUCKETS - max_exact)).astype(jnp.int32)
    large = jnp.minimum(large, REL_BUCKETS - 1)
    return jnp.where(dist < max_exact, dist, large)


def _prompt_bias(rel_bias, gi):
    dil, reach = DILATIONS[gi], WINDOWS[gi] // DILATIONS[gi]
    blk = ATTN_BLOCK
    step = (jnp.arange(blk)[:, None] + blk) - jnp.arange(2 * blk)[None, :]
    tab = rel_bias[:, gi * HEADS_PER_GROUP:(gi + 1) * HEADS_PER_GROUP]
    bias = tab[_rel_bucket(jnp.maximum(step, 0) * dil)].transpose(2, 0, 1).astype(F32)
    valid = (step >= 0) & (step <= reach)
    first = valid & (jnp.arange(2 * blk)[None, :] >= blk)
    neg = jnp.float32(-jnp.inf)
    return jnp.stack([jnp.where(first[None], bias, neg), jnp.where(valid[None], bias, neg)])


def _sample_bias(rel_bias, gi, rows_c, row_pos):
    dil, window = DILATIONS[gi], WINDOWS[gi]
    tab = rel_bias[:, gi * HEADS_PER_GROUP:(gi + 1) * HEADS_PER_GROUP]
    seg_row = jnp.arange(SEG)
    is_tok = (seg_row >= SEG_T0) & (seg_row < SEG_T0 + 4)
    t = jnp.where(is_tok, seg_row - SEG_T0, 0)
    neg = jnp.float32(-jnp.inf)

    def table(dist, ok):
        ok = ok & (dist >= 0) & (dist % dil == 0) & (dist // dil <= window // dil)
        bias = tab[_rel_bucket(jnp.maximum(dist, 0))].transpose(2, 0, 1).astype(F32)
        return jnp.where(ok[None], bias, neg)

    dist_c = t[:, None] - row_pos[None, :]
    bias_c = table(dist_c, jnp.ones(dist_c.shape, bool))
    key_tok = seg_row - SEG_T0
    dist_n = t[:, None] - key_tok[None, :]
    bias_n = table(dist_n, jnp.broadcast_to(is_tok[None, :], dist_n.shape))
    return bias_c, bias_n


def _ssd_mats(rows, seg_len):
    l = jnp.arange(rows)[:, None]
    s = jnp.arange(rows)[None, :]
    same = (l // seg_len) == (s // seg_len)
    causal = same & (s <= l)
    return causal.astype(F32), same.astype(F32), causal.astype(F32)


def kernel(x_prompt, x_sample, cache_kv_w128, cache_kv_w512, cache_kv_w2048, state_conv, state_ssm, rel_bias,
           ffn1_norm, ffn1_w13, ffn1_w2, mix_norm, w_in, conv_w, conv_b, dt_bias, a_log, d_skip, ssd_norm,
           w_ssd_out, w_attn_out, w_o, ffn2_norm, ffn2_w13, ffn2_w2, final_norm):
    batch, seq, _ = x_prompt.shape
    dec_batch, dec_seq, _ = x_sample.shape
    n_p = batch * seq
    n_s = dec_batch * SEG
    tm_p, tm_s = 512, min(512, n_s)
    tok_lo, tok_hi = SEG_T0, SEG_T0 + dec_seq

    z_w, xbc_w, dt_w, q_w, k_w, v_w, gs_w, ga_w = jnp.split(
        w_in, [2048, 6144, 6176, 6944, 7712, 8480, 9504], axis=-1)
    qkv_w = []
    for gi in range(N_DIL_GROUPS):
        sl = slice(gi * ATTN_GROUP_WIDTH, (gi + 1) * ATTN_GROUP_WIDTH)
        qkv_w += [q_w[..., sl], k_w[..., sl], v_w[..., sl]]
    pad_w = jnp.zeros((DEPTH, D_MODEL, PROJ_W - COL_DT - SSD_HEADS), w_in.dtype)
    w_in_p = jnp.concatenate([xbc_w, z_w, gs_w, ga_w] + qkv_w + [dt_w, pad_w], axis=-1).astype(BF16)

    def pad_heads(v):
        return jnp.pad(v, ((0, 0), (0, LANES - SSD_HEADS)))[:, None, :]

    layers = []
    for l in range(DEPTH):
        layers.append(dict(
            ffn1_norm=ffn1_norm[l][None], ffn1_w13=ffn1_w13[l].astype(BF16), ffn1_w2=ffn1_w2[l].astype(BF16),
            mix_norm=mix_norm[l][None], w_in=w_in_p[l],
            conv_w=conv_w[l], conv_b=conv_b[l][None],
            dt_bias=pad_heads(dt_bias)[l], a_log=pad_heads(a_log)[l],
            d_skip=jnp.repeat(d_skip[l], SSD_HEAD_DIM)[None], ssd_norm=ssd_norm[l][None],
            w_ssd_out=w_ssd_out[l].astype(BF16), w_attn_out=w_attn_out[l].astype(BF16),
            w_o=w_o[l].astype(BF16),
            ffn2_norm=ffn2_norm[l][None], ffn2_w13=ffn2_w13[l].astype(BF16), ffn2_w2=ffn2_w2[l].astype(BF16)))

    mats_p = _ssd_mats(SSD_CHUNK, SSD_CHUNK)
    rows_s = 8 * SEG
    mats_s = _ssd_mats(rows_s, SEG)
    bias_p = [_prompt_bias(rel_bias, gi) for gi in range(N_DIL_GROUPS)]
    caches = (cache_kv_w128, cache_kv_w512, cache_kv_w2048)
    bias_s, caches_c = [], []
    for gi in range(N_DIL_GROUPS):
        lb = caches[gi].shape[2]
        row_pos = jnp.arange(lb) - lb
        bias_s.append(_sample_bias(rel_bias, gi, lb, row_pos))
        caches_c.append(caches[gi].reshape(DEPTH, dec_batch, lb, 2 * ATTN_GROUP_WIDTH))
    conv_state_rows = jnp.pad(state_conv, ((0, 0), (0, 0), (0, SEG - (CONV_WIDTH - 1)), (0, 0))).reshape(
        DEPTH, n_s, CONV_DIM)
    sample_bt = tuple(min(bt, dec_batch) for bt in (32, 8, 2))

    xp = x_prompt.reshape(n_p, D_MODEL)
    xs = jnp.pad(x_sample, ((0, 0), (SEG_T0, SEG - SEG_T0 - dec_seq), (0, 0))).reshape(n_s, D_MODEL)

    kv_p = [[], [], []]
    kv_s = [[], [], []]
    conv_p, conv_s, ssm_p, ssm_s = [], [], [], []
    for l in range(DEPTH):
        lw = layers[l]
        xp = _ffn(xp, lw["ffn1_norm"], lw["ffn1_w13"], lw["ffn1_w2"], tm_p)
        xs = _ffn(xs, lw["ffn1_norm"], lw["ffn1_w13"], lw["ffn1_w2"], tm_s)
        proj_p = _inproj(xp, lw["mix_norm"], lw["w_in"], tm_p)
        proj_s = _inproj(xs, lw["mix_norm"], lw["w_in"], tm_s)

        y_p, st_p = _ssd(proj_p, lw, mats_p, batch, seq // SSD_CHUNK, SSD_CHUNK, SSD_CHUNK, 0, SSD_CHUNK)
        y_s, st_s = _ssd(proj_s, lw, mats_s, n_s // rows_s, 1, rows_s, SEG, tok_lo, tok_hi,
                         conv_state=conv_state_rows[l], ssm_state=state_ssm[l])
        attn_p = [_attn_prompt(proj_p, bias_p[gi], gi, batch, seq) for gi in range(N_DIL_GROUPS)]
        attn_s = [_attn_sample(proj_s, caches_c[gi][l], bias_s[gi][0], bias_s[gi][1], gi, sample_bt[gi])
                  for gi in range(N_DIL_GROUPS)]

        xp = _mix_out(xp, y_p, proj_p, attn_p, lw, tm_p)
        xs = _mix_out(xs, y_s, proj_s, attn_s, lw, tm_s, seg_len=SEG)
        xp = _ffn(xp, lw["ffn2_norm"], lw["ffn2_w13"], lw["ffn2_w2"], tm_p)
        xs = _ffn(xs, lw["ffn2_norm"], lw["ffn2_w13"], lw["ffn2_w2"], tm_s)

        pp = proj_p.reshape(batch, seq, PROJ_W)
        ps = proj_s.reshape(dec_batch, SEG, PROJ_W)
        conv_p.append(pp[:, seq - (CONV_WIDTH - 1):, COL_XBC:COL_XBC + CONV_DIM])
        conv_s.append(ps[:, tok_hi - (CONV_WIDTH - 1):tok_hi, COL_XBC:COL_XBC + CONV_DIM])
        ssm_p.append(st_p)
        ssm_s.append(st_s)
        for gi in range(N_DIL_GROUPS):
            c0 = COL_QKV + gi * ATTN_WIDTH + ATTN_GROUP_WIDTH
            keep = min(WINDOWS[gi], seq)
            kv_p[gi].append(pp[:, seq - keep:, c0:c0 + 2 * ATTN_GROUP_WIDTH].reshape(
                batch, keep, 2, HEADS_PER_GROUP, ATTN_HEAD_DIM))
            kv_s[gi].append(ps[:, tok_lo:tok_hi, c0:c0 + 2 * ATTN_GROUP_WIDTH].reshape(
                dec_batch, dec_seq, 2, HEADS_PER_GROUP, ATTN_HEAD_DIM))

    fn = final_norm[None]
    y_prompt = _final_norm(xp, fn, tm_p).reshape(batch, seq, D_MODEL)
    y_sample = _final_norm(xs, fn, tm_s).reshape(dec_batch, SEG, D_MODEL)[:, tok_lo:tok_hi]
    return (y_prompt, y_sample,
            jnp.stack(kv_p[0]), jnp.stack(kv_p[1]), jnp.stack(kv_p[2]), jnp.stack(conv_p), jnp.stack(ssm_p),
            jnp.stack(kv_s[0]), jnp.stack(kv_s[1]), jnp.stack(kv_s[2]), jnp.stack(conv_s), jnp.stack(ssm_s))
```

```python
import functools
import math

import jax
import jax.numpy as jnp
from jax import lax
from jax.experimental import pallas as pl
from jax.experimental.pallas import tpu as pltpu

F32 = jnp.float32
BF16 = jnp.bfloat16

D_MODEL = 1024
DEPTH = 4
D_INNER = 2048
SSD_HEAD_DIM = 64
SSD_HEADS = 32
SSD_GROUPS = 8
SSD_HPG = 4
D_STATE = 128
CONV_WIDTH = 4
CONV_DIM = 4096
SSD_CHUNK = 128
ATTN_HEAD_DIM = 64
HEADS_PER_GROUP = 4
WINDOWS = (128, 512, 2048)
DILATIONS = (1, 4, 16)
N_DIL_GROUPS = 3
ATTN_WIDTH = 768
ATTN_GROUP_WIDTH = HEADS_PER_GROUP * ATTN_HEAD_DIM
KV_WIDTH = 2 * ATTN_GROUP_WIDTH
ATTN_BLOCK = 128
ATTN_SCALE = ATTN_HEAD_DIM ** -0.5
REL_BUCKETS = 32
REL_MAX_DIST = 2048
D_FF = 2816
FFN_RES = 0.5
NORM_EPS = 1e-6

LANES = 128
SEG = 8
SEG_T0 = CONV_WIDTH - 1

COL_XBC = 0
COL_Z = COL_XBC + CONV_DIM
COL_GS = COL_Z + D_INNER
COL_GA = COL_GS + D_MODEL
COL_QKV = COL_GA + D_MODEL
COL_DT = COL_QKV + 3 * ATTN_WIDTH
PROJ_W = 10752
PROJ_TN = 768

VMEM_LIMIT = 56 * 1024 * 1024


def _cparams(sem):
    return pltpu.CompilerParams(dimension_semantics=sem, vmem_limit_bytes=VMEM_LIMIT)


def _layer_spec(l, shape, single=False):
    nd = len(shape)
    kw = dict(pipeline_mode=pl.Buffered(1)) if single else {}
    return pl.BlockSpec((None,) + tuple(shape), lambda *_: (l,) + (0,) * nd, **kw)


def _const_spec(shape):
    nd = len(shape)
    return pl.BlockSpec(tuple(shape), lambda *_: (0,) * nd)


_ANY = pl.BlockSpec(memory_space=pl.ANY)


def _rms(x, g):
    return x * lax.rsqrt(jnp.mean(x * x, axis=-1, keepdims=True) + NORM_EPS) * g


def _silu(x):
    return x * jax.nn.sigmoid(x)


def _dot(a, b):
    return jnp.dot(a, b, preferred_element_type=F32)


def _dot_nt(a, b):
    return lax.dot_general(a, b, (((1,), (1,)), ((), ())), preferred_element_type=F32)


def _dot_tn(a, b):
    return lax.dot_general(a, b, (((0,), (0,)), ((), ())), preferred_element_type=F32)


def _ffn_kernel(x_ref, g_ref, w13_ref, w2_ref, o_ref, *, n_chunks):
    x = x_ref[...]
    xn = _rms(x, g_ref[...]).astype(BF16)
    tf = D_FF // n_chunks
    acc = None
    for c in range(n_chunks):
        a = _dot(xn, w13_ref[:, c * tf:(c + 1) * tf])
        b = _dot(xn, w13_ref[:, D_FF + c * tf:D_FF + (c + 1) * tf])
        h = (_silu(a) * b).astype(BF16)
        d = _dot(h, w2_ref[c * tf:(c + 1) * tf, :])
        acc = d if acc is None else acc + d
    o_ref[...] = x + FFN_RES * acc


def _ffn(x, l, g, w13, w2, tm):
    n = x.shape[0]
    return pl.pallas_call(
        functools.partial(_ffn_kernel, n_chunks=2),
        grid=(n // tm,),
        in_specs=[pl.BlockSpec((tm, D_MODEL), lambda i: (i, 0)),
                  _layer_spec(l, (1, D_MODEL)),
                  _layer_spec(l, (D_MODEL, 2 * D_FF), single=True),
                  _layer_spec(l, (D_FF, D_MODEL), single=True)],
        out_specs=pl.BlockSpec((tm, D_MODEL), lambda i: (i, 0)),
        out_shape=jax.ShapeDtypeStruct((n, D_MODEL), F32),
        compiler_params=_cparams(("parallel",)),
        name="ffn",
    )(x, g, w13, w2)


def _inproj_kernel(x_ref, g_ref, w_ref, o_ref, xn_ref):
    @pl.when(pl.program_id(1) == 0)
    def _():
        xn_ref[...] = _rms(x_ref[...], g_ref[...]).astype(BF16)

    o_ref[...] = _dot_nt(xn_ref[...], w_ref[...])


def _inproj(x, l, g, w_t, tm):
    n = x.shape[0]
    return pl.pallas_call(
        _inproj_kernel,
        grid=(n // tm, PROJ_W // PROJ_TN),
        in_specs=[pl.BlockSpec((tm, D_MODEL), lambda i, j: (i, 0)),
                  _layer_spec(l, (1, D_MODEL)),
                  pl.BlockSpec((None, PROJ_TN, D_MODEL), lambda i, j: (l, j, 0))],
        out_specs=pl.BlockSpec((tm, PROJ_TN), lambda i, j: (i, j)),
        out_shape=jax.ShapeDtypeStruct((n, PROJ_W), F32),
        scratch_shapes=[pltpu.VMEM((tm, D_MODEL), BF16)],
        compiler_params=_cparams(("parallel", "arbitrary")),
        name="inproj",
    )(x, g, w_t)


def _ssd_kernel(*refs, rows, seg_len, has_state, token_lo, token_hi):
    (xbc_ref, z_ref, dt_ref, cw_ref, cb_ref, dtb_ref, alog_ref, dskip_ref, norm_ref,
     csm_ref, totm_ref, mask_ref) = refs[:12]
    rest = refs[12:]
    if has_state:
        cst_ref, st_in_ref = rest[:2]
        rest = rest[2:]
    _, y_ref, st_ref, xpad_ref, conv_ref, yscr_ref = rest
    nseg = rows // seg_len
    c = pl.program_id(1)

    @pl.when(c == 0)
    def _():
        xpad_ref[0:8, :] = jnp.zeros((8, CONV_DIM), F32)
        if has_state:
            st_ref[...] = st_in_ref[...]
        else:
            st_ref[...] = jnp.zeros(st_ref.shape, F32)

    row_in_seg = lax.broadcasted_iota(jnp.int32, (rows, 1), 0) % seg_len
    xbc = xbc_ref[...]
    if has_state:
        xbc = jnp.where(row_in_seg < SEG_T0, cst_ref[...], xbc)
    xpad_ref[8:8 + rows, :] = xbc
    conv = cb_ref[...] + xbc * cw_ref[CONV_WIDTH - 1:CONV_WIDTH, :]
    for i in range(CONV_WIDTH - 1):
        conv = conv + xpad_ref[5 + i:5 + i + rows, :] * cw_ref[i:i + 1, :]
    xpad_ref[0:8, :] = xbc[rows - 8:rows, :]
    conv_ref[...] = _silu(conv)

    is_token = (row_in_seg >= token_lo) & (row_in_seg < token_hi)
    dtr = dt_ref[...] + dtb_ref[...]
    dt = jnp.maximum(dtr, 0.0) + jnp.log1p(jnp.exp(-jnp.abs(dtr)))
    dt = jnp.where(is_token, dt, 0.0)
    a = dt * (-jnp.exp(alog_ref[...]))
    a_cs = jnp.dot(csm_ref[...], a, preferred_element_type=F32, precision=lax.Precision.HIGHEST)
    a_tot = jnp.dot(totm_ref[...], a, preferred_element_type=F32, precision=lax.Precision.HIGHEST)
    if rows < LANES:
        a_cs_sq = jnp.concatenate([a_cs, jnp.zeros((LANES - rows, LANES), F32)], axis=0)
    else:
        a_cs_sq = a_cs
    a_cs_t = a_cs_sq.T
    to_end = jnp.exp(a_tot - a_cs)
    ea = jnp.exp(a_cs)
    cdec = jnp.exp(a_tot)
    mask = mask_ref[...] > 0.5

    for g in range(SSD_GROUPS):
        bg = conv_ref[:, D_INNER + g * D_STATE:D_INNER + (g + 1) * D_STATE].astype(BF16)
        cg = conv_ref[:, D_INNER + SSD_GROUPS * D_STATE + g * D_STATE:
                      D_INNER + SSD_GROUPS * D_STATE + (g + 1) * D_STATE].astype(BF16)
        cbm = _dot_nt(cg, bg)
        w_parts = []
        for r in range(SSD_HPG):
            h = g * SSD_HPG + r
            seg = a_cs[:, h:h + 1] - a_cs_t[h:h + 1, 0:rows]
            decay = jnp.exp(jnp.where(mask, seg, -jnp.inf))
            m_h = (cbm * decay).astype(BF16)
            xdt = conv_ref[:, h * SSD_HEAD_DIM:(h + 1) * SSD_HEAD_DIM] * dt[:, h:h + 1]
            yscr_ref[:, h * SSD_HEAD_DIM:(h + 1) * SSD_HEAD_DIM] = _dot(m_h, xdt.astype(BF16))
            w_parts.append((xdt * to_end[:, h:h + 1]).astype(BF16))
        w_g = jnp.concatenate(w_parts, axis=1)
        for j in range(nseg):
            r0 = j * seg_len
            s_old = st_ref[j, g * SSD_HPG:(g + 1) * SSD_HPG].reshape(SSD_HPG * SSD_HEAD_DIM, D_STATE)
            yo = _dot_nt(cg[r0:r0 + seg_len], s_old.astype(BF16))
            dec_parts = []
            for r in range(SSD_HPG):
                h = g * SSD_HPG + r
                lo = h * SSD_HEAD_DIM
                yscr_ref[r0:r0 + seg_len, lo:lo + SSD_HEAD_DIM] += (
                    yo[:, r * SSD_HEAD_DIM:(r + 1) * SSD_HEAD_DIM] * ea[r0:r0 + seg_len, h:h + 1])
                dec_parts.append(jnp.broadcast_to(cdec[r0:r0 + 1, h:h + 1], (SSD_HEAD_DIM, D_STATE)))
            dec = jnp.concatenate(dec_parts, axis=0)
            s_new = s_old * dec + _dot_tn(w_g[r0:r0 + seg_len], bg[r0:r0 + seg_len])
            st_ref[j, g * SSD_HPG:(g + 1) * SSD_HPG] = s_new.reshape(SSD_HPG, SSD_HEAD_DIM, D_STATE)

    gw = D_INNER // SSD_GROUPS
    for g in range(SSD_GROUPS):
        sl = slice(g * gw, (g + 1) * gw)
        y = yscr_ref[:, sl] + dskip_ref[:, sl] * conv_ref[:, sl]
        y = y * _silu(z_ref[:, sl])
        y = y * lax.rsqrt(jnp.mean(y * y, axis=-1, keepdims=True) + NORM_EPS) * norm_ref[:, sl]
        y_ref[:, sl] = y.astype(BF16)


def _ssd(proj, l, prm, mats, st_all, n_outer, n_inner, rows, seg_len, token_lo, token_hi,
         conv_state=None, ssm_state=None):
    has_state = ssm_state is not None
    nseg = rows // seg_len
    n = proj.shape[0]
    st_block = (None, nseg, SSD_HEADS, SSD_HEAD_DIM, D_STATE)

    def rowmap(cb):
        return lambda o, c: (o * n_inner + c, cb)

    def st_map(o, c):
        return (l, o, 0, 0, 0)

    in_specs = [pl.BlockSpec((rows, CONV_DIM), rowmap(COL_XBC // CONV_DIM)),
                pl.BlockSpec((rows, D_INNER), rowmap(COL_Z // D_INNER)),
                pl.BlockSpec((rows, LANES), rowmap(COL_DT // LANES)),
                _layer_spec(l, (CONV_WIDTH, CONV_DIM)), _layer_spec(l, (1, CONV_DIM)),
                _layer_spec(l, (1, LANES)), _layer_spec(l, (1, LANES)),
                _layer_spec(l, (1, D_INNER)), _layer_spec(l, (1, D_INNER)),
                _const_spec((rows, rows)), _const_spec((rows, rows)), _const_spec((rows, rows))]
    args = [proj, proj, proj, prm["conv_w"], prm["conv_b"], prm["dt_bias"], prm["a_log"],
            prm["d_skip"], prm["ssd_norm"], mats[0], mats[1], mats[2]]
    if has_state:
        in_specs += [pl.BlockSpec((None, rows, CONV_DIM), lambda o, c: (l, o * n_inner + c, 0)),
                     pl.BlockSpec(st_block, st_map)]
        args += [conv_state, ssm_state]
    in_specs.append(_ANY)
    args.append(st_all)
    y, st = pl.pallas_call(
        functools.partial(_ssd_kernel, rows=rows, seg_len=seg_len, has_state=has_state,
                          token_lo=token_lo, token_hi=token_hi),
        grid=(n_outer, n_inner),
        in_specs=in_specs,
        out_specs=[pl.BlockSpec((rows, D_INNER), rowmap(0)), pl.BlockSpec(st_block, st_map)],
        out_shape=[jax.ShapeDtypeStruct((n, D_INNER), BF16),
                   jax.ShapeDtypeStruct(st_all.shape, F32)],
        scratch_shapes=[pltpu.VMEM((8 + rows, CONV_DIM), F32),
                        pltpu.VMEM((rows, CONV_DIM), F32),
                        pltpu.VMEM((rows, D_INNER), F32)],
        input_output_aliases={len(args) - 1: 1},
        compiler_params=_cparams(("parallel", "arbitrary")),
        name="ssd_state" if has_state else "ssd_prompt",
    )(*args)
    return y, st


def _attn_prompt_kernel(q0, q1, k0, k1, v0, v1, bias_ref, _, o_ref, lse_ref, kvt_ref, *, dil, seq, keep):
    nblk = seq // (ATTN_BLOCK * dil)
    lo_lanes = lax.broadcasted_iota(jnp.int32, (ATTN_BLOCK, LANES), 1) < ATTN_HEAD_DIM

    tc = min(keep, 512)
    for half, (k, v) in enumerate(((k0, v0), (k1, v1))):
        for c0 in range(0, keep, tc):
            src = pl.ds(seq - keep + c0, tc)
            kvt_ref[half * LANES:(half + 1) * LANES, c0:c0 + tc] = k[src, :].T
            kvt_ref[ATTN_GROUP_WIDTH + half * LANES:ATTN_GROUP_WIDTH + (half + 1) * LANES, c0:c0 + tc] = v[src, :].T

    def rows(start):
        if dil == 1:
            return pl.ds(pl.multiple_of(start, ATTN_BLOCK), ATTN_BLOCK)
        return pl.ds(start, ATTN_BLOCK, stride=dil)

    def body(i, carry):
        r = i // nblk
        blk = i % nblk
        cur = rows(r + blk * ATTN_BLOCK * dil)
        prev = rows(r + jnp.maximum(blk - 1, 0) * ATTN_BLOCK * dil)
        tab = jnp.minimum(blk, 1)
        for half, (q, k, v) in enumerate(((q0, k0, v0), (q1, k1, v1))):
            qb = q[cur, :]
            kk = jnp.concatenate([k[prev, :], k[cur, :]], axis=0).astype(BF16)
            vv = jnp.concatenate([v[prev, :], v[cur, :]], axis=0).astype(BF16)
            outs, lses = [], []
            for hh in range(2):
                lanes = lo_lanes if hh == 0 else jnp.logical_not(lo_lanes)
                qm = jnp.where(lanes, qb, 0.0).astype(BF16)
                s = _dot_nt(qm, kk) * ATTN_SCALE + bias_ref[tab, 2 * half + hh]
                m = jnp.max(s, axis=-1, keepdims=True)
                p = jnp.exp(s - m)
                l = jnp.sum(p, axis=-1, keepdims=True)
                outs.append(_dot(p.astype(BF16), vv) / l)
                lses.append(jnp.broadcast_to(m + jnp.log(l), (ATTN_BLOCK, LANES)))
            o_ref[half, cur, :] = jnp.where(lo_lanes, outs[0], outs[1])
            lse_ref[half, cur, :] = jnp.where(lo_lanes, lses[0], lses[1])
        return carry

    lax.fori_loop(0, dil * nblk, body, 0)


def _attn_prompt(proj, l, bias, kvt_all, gi, batch, seq):
    dil = DILATIONS[gi]
    keep = kvt_all.shape[-1]
    base = (COL_QKV + gi * ATTN_WIDTH) // LANES

    def col(j):
        return pl.BlockSpec((seq, LANES), lambda b: (b, base + j))

    out_spec = pl.BlockSpec((2, seq, LANES), lambda b: (0, b, 0))
    shape = jax.ShapeDtypeStruct((2, batch * seq, LANES), F32)
    return pl.pallas_call(
        functools.partial(_attn_prompt_kernel, dil=dil, seq=seq, keep=keep),
        grid=(batch,),
        in_specs=[col(0), col(1), col(2), col(3), col(4), col(5),
                  _const_spec((2, HEADS_PER_GROUP, ATTN_BLOCK, 2 * ATTN_BLOCK)), _ANY],
        out_specs=[out_spec, out_spec,
                   pl.BlockSpec((None, None, KV_WIDTH, keep), lambda b: (l, b, 0, 0))],
        out_shape=[shape, shape, jax.ShapeDtypeStruct(kvt_all.shape, F32)],
        input_output_aliases={7: 2},
        compiler_params=_cparams(("parallel",)),
        name=f"attn_prompt_d{dil}",
    )(proj, proj, proj, proj, proj, proj, bias, kvt_all)


def _attn_sample_kernel(q0, q1, kn0, kn1, vn0, vn1, cache_ref, bias_c_ref, bias_n_ref, o_ref, lse_ref, *, bt):
    lo8 = lax.broadcasted_iota(jnp.int32, (SEG, LANES), 1) < ATTN_HEAD_DIM
    lo16 = lax.broadcasted_iota(jnp.int32, (2 * SEG, LANES), 1) < ATTN_HEAD_DIM
    top16 = lax.broadcasted_iota(jnp.int32, (2 * SEG, LANES), 0) < SEG
    head_lanes = lo16 == top16

    def body(b, carry):
        seg = pl.ds(pl.multiple_of(b * SEG, SEG), SEG)
        for half, (q, kn, vn) in enumerate(((q0, kn0, vn0), (q1, kn1, vn1))):
            qb = q[seg, :]
            q16 = jnp.where(head_lanes, jnp.concatenate([qb, qb], axis=0), 0.0).astype(BF16)
            knew = kn[seg, :].astype(BF16)
            vnew = vn[seg, :].astype(BF16)
            kt = cache_ref[b, half * LANES:(half + 1) * LANES, :].astype(BF16)
            vt = cache_ref[b, ATTN_GROUP_WIDTH + half * LANES:
                           ATTN_GROUP_WIDTH + (half + 1) * LANES, :].astype(BF16)
            sc = _dot(q16, kt) * ATTN_SCALE + bias_c_ref[half]
            sn = _dot_nt(q16, knew) * ATTN_SCALE + bias_n_ref[half]
            m = jnp.maximum(jnp.max(sc, axis=-1, keepdims=True), jnp.max(sn, axis=-1, keepdims=True))
            pc = jnp.exp(sc - m)
            pn = jnp.exp(sn - m)
            l = jnp.sum(pc, axis=-1, keepdims=True) + jnp.sum(pn, axis=-1, keepdims=True)
            o = (_dot_nt(pc.astype(BF16), vt) + _dot(pn.astype(BF16), vnew)) / l
            lse = jnp.broadcast_to(m + jnp.log(l), (2 * SEG, LANES))
            o_ref[half, seg, :] = jnp.where(lo8, o[0:SEG], o[SEG:2 * SEG])
            lse_ref[half, seg, :] = jnp.where(lo8, lse[0:SEG], lse[SEG:2 * SEG])
        return carry

    lax.fori_loop(0, bt, body, 0)


def _attn_sample(proj, l, cache_t, bias_c, bias_n, gi, bt):
    nb = cache_t.shape[1]
    rows_c = cache_t.shape[3]
    base = (COL_QKV + gi * ATTN_WIDTH) // LANES

    def col(j):
        return pl.BlockSpec((bt * SEG, LANES), lambda i: (i, base + j))

    out_spec = pl.BlockSpec((2, bt * SEG, LANES), lambda i: (0, i, 0))
    shape = jax.ShapeDtypeStruct((2, nb * SEG, LANES), F32)
    return pl.pallas_call(
        functools.partial(_attn_sample_kernel, bt=bt),
        grid=(nb // bt,),
        in_specs=[col(0), col(1), col(2), col(3), col(4), col(5),
                  pl.BlockSpec((None, bt, KV_WIDTH, rows_c), lambda i: (l, i, 0, 0)),
                  _const_spec((2, 2 * SEG, rows_c)),
                  _const_spec((2, 2 * SEG, SEG))],
        out_specs=[out_spec, out_spec],
        out_shape=[shape, shape],
        compiler_params=_cparams(("parallel",)),
        name=f"attn_sample_d{DILATIONS[gi]}",
    )(proj, proj, proj, proj, proj, proj, cache_t, bias_c, bias_n)


def _mix_out_kernel(x_ref, y_ref, gs_ref, ga_ref, o0, o1, o2, l0, l1, l2, wso_ref, wao_ref, wo_ref, out_ref,
                    *, token_lo, token_hi, seg_len):
    ssd_out = _dot(y_ref[...], wso_ref[...])
    attn_out = None
    for half in range(2):
        ls = [l0[half], l1[half], l2[half]]
        os_ = [o0[half], o1[half], o2[half]]
        m = jnp.maximum(jnp.maximum(ls[0], ls[1]), ls[2])
        es = [jnp.exp(l - m) for l in ls]
        den = es[0] + es[1] + es[2]
        o = (es[0] / den) * os_[0] + (es[1] / den) * os_[1] + (es[2] / den) * os_[2]
        d = _dot(o.astype(BF16), wao_ref[half * LANES:(half + 1) * LANES, :])
        attn_out = d if attn_out is None else attn_out + d
    merged = jax.nn.sigmoid(gs_ref[...]) * ssd_out + jax.nn.sigmoid(ga_ref[...]) * attn_out
    upd = _dot(merged.astype(BF16), wo_ref[...])
    if seg_len is not None:
        rows = x_ref.shape[0]
        ris = lax.broadcasted_iota(jnp.int32, (rows, 1), 0) % seg_len
        upd = jnp.where((ris >= token_lo) & (ris < token_hi), upd, 0.0)
    out_ref[...] = x_ref[...] + upd


def _mix_out(x, y, proj, attn, l, prm, tm, seg_len=None):
    n = x.shape[0]
    half_spec = pl.BlockSpec((2, tm, LANES), lambda i: (0, i, 0))
    (o0, l0), (o1, l1), (o2, l2) = attn
    return pl.pallas_call(
        functools.partial(_mix_out_kernel, token_lo=SEG_T0, token_hi=SEG_T0 + 4, seg_len=seg_len),
        grid=(n // tm,),
        in_specs=[pl.BlockSpec((tm, D_MODEL), lambda i: (i, 0)),
                  pl.BlockSpec((tm, D_INNER), lambda i: (i, 0)),
                  pl.BlockSpec((tm, D_MODEL), lambda i: (i, COL_GS // D_MODEL)),
                  pl.BlockSpec((tm, D_MODEL), lambda i: (i, COL_GA // D_MODEL)),
                  half_spec, half_spec, half_spec, half_spec, half_spec, half_spec,
                  _layer_spec(l, (D_INNER, D_MODEL), single=True),
                  _layer_spec(l, (ATTN_GROUP_WIDTH, D_MODEL), single=True),
                  _layer_spec(l, (D_MODEL, D_MODEL), single=True)],
        out_specs=pl.BlockSpec((tm, D_MODEL), lambda i: (i, 0)),
        out_shape=jax.ShapeDtypeStruct((n, D_MODEL), F32),
        compiler_params=_cparams(("parallel",)),
        name="mix_out",
    )(x, y, proj, proj, o0, o1, o2, l0, l1, l2, prm["w_ssd_out"], prm["w_attn_out"], prm["w_o"])


def _final_norm_kernel(x_ref, g_ref, o_ref):
    o_ref[...] = _rms(x_ref[...], g_ref[...])


def _final_norm(x, g, tm):
    n = x.shape[0]
    return pl.pallas_call(
        _final_norm_kernel,
        grid=(n // tm,),
        in_specs=[pl.BlockSpec((tm, D_MODEL), lambda i: (i, 0)),
                  pl.BlockSpec((1, D_MODEL), lambda i: (0, 0))],
        out_specs=pl.BlockSpec((tm, D_MODEL), lambda i: (i, 0)),
        out_shape=jax.ShapeDtypeStruct((n, D_MODEL), F32),
        compiler_params=_cparams(("parallel",)),
        name="final_norm",
    )(x, g)


def _rel_bucket(dist):
    max_exact = REL_BUCKETS // 2
    d = jnp.maximum(dist, 1).astype(F32)
    large = max_exact + (jnp.log(d / max_exact) / math.log(REL_MAX_DIST / max_exact)
                         * (REL_BUCKETS - max_exact)).astype(jnp.int32)
    large = jnp.minimum(large, REL_BUCKETS - 1)
    return jnp.where(dist < max_exact, dist, large)


def _bias_lookup(tab, dist):
    onehot = (_rel_bucket(dist)[..., None] == jnp.arange(REL_BUCKETS)).astype(F32)
    return jnp.einsum("...k,kh->h...", onehot, tab.astype(F32), precision=lax.Precision.HIGHEST)


def _prompt_bias(rel_bias, gi):
    dil, reach = DILATIONS[gi], WINDOWS[gi] // DILATIONS[gi]
    blk = ATTN_BLOCK
    step = (jnp.arange(blk)[:, None] + blk) - jnp.arange(2 * blk)[None, :]
    tab = rel_bias[:, gi * HEADS_PER_GROUP:(gi + 1) * HEADS_PER_GROUP]
    bias = _bias_lookup(tab, jnp.maximum(step, 0) * dil)
    valid = (step >= 0) & (step <= reach)
    first = valid & (jnp.arange(2 * blk)[None, :] >= blk)
    neg = jnp.float32(-jnp.inf)
    return jnp.stack([jnp.where(first[None], bias, neg), jnp.where(valid[None], bias, neg)])


def _sample_bias(rel_bias, gi, row_pos):
    dil, window = DILATIONS[gi], WINDOWS[gi]
    tab = rel_bias[:, gi * HEADS_PER_GROUP:(gi + 1) * HEADS_PER_GROUP]
    seg_row = jnp.arange(SEG)
    is_tok = (seg_row >= SEG_T0) & (seg_row < SEG_T0 + 4)
    t = jnp.where(is_tok, seg_row - SEG_T0, 0)
    neg = jnp.float32(-jnp.inf)

    def table(dist, ok):
        ok = ok & (dist >= 0) & (dist % dil == 0) & (dist // dil <= window // dil)
        tbl = jnp.where(ok[None], _bias_lookup(tab, jnp.maximum(dist, 0)), neg)
        return tbl.reshape(2, 2 * SEG, dist.shape[1])

    dist_c = t[:, None] - row_pos[None, :]
    bias_c = table(dist_c, jnp.ones(dist_c.shape, bool))
    key_tok = seg_row - SEG_T0
    dist_n = t[:, None] - key_tok[None, :]
    bias_n = table(dist_n, jnp.broadcast_to(is_tok[None, :], dist_n.shape))
    return bias_c, bias_n


def _ssd_mats(rows, seg_len):
    l = jnp.arange(rows)[:, None]
    s = jnp.arange(rows)[None, :]
    same = (l // seg_len) == (s // seg_len)
    causal = same & (s <= l)
    return causal.astype(F32), same.astype(F32), causal.astype(F32)


def _kv_rows_minor(c):
    d, b, rows = c.shape[:3]
    return c.transpose(0, 1, 3, 4, 5, 2).reshape(d, b, KV_WIDTH, rows)


def kernel(x_prompt, x_sample, cache_kv_w128, cache_kv_w512, cache_kv_w2048, state_conv, state_ssm, rel_bias,
           ffn1_norm, ffn1_w13, ffn1_w2, mix_norm, w_in, conv_w, conv_b, dt_bias, a_log, d_skip, ssd_norm,
           w_ssd_out, w_attn_out, w_o, ffn2_norm, ffn2_w13, ffn2_w2, final_norm):
    batch, seq, _ = x_prompt.shape
    dec_batch, dec_seq, _ = x_sample.shape
    n_p = batch * seq
    n_s = dec_batch * SEG
    tm_p, tm_s = 512, min(512, n_s)
    tmi_p, tmi_s = 2048, min(1024, n_s)
    tok_lo, tok_hi = SEG_T0, SEG_T0 + dec_seq

    w_t = jnp.swapaxes(w_in, 1, 2)
    z_w, xbc_w, dt_w, q_w, k_w, v_w, gs_w, ga_w = jnp.split(
        w_t, [2048, 6144, 6176, 6944, 7712, 8480, 9504], axis=1)
    qkv_w = []
    for gi in range(N_DIL_GROUPS):
        sl = slice(gi * ATTN_GROUP_WIDTH, (gi + 1) * ATTN_GROUP_WIDTH)
        qkv_w += [q_w[:, sl], k_w[:, sl], v_w[:, sl]]
    pad_w = jnp.zeros((DEPTH, PROJ_W - COL_DT - SSD_HEADS, D_MODEL), w_in.dtype)
    w_in_t = jnp.concatenate([xbc_w, z_w, gs_w, ga_w] + qkv_w + [dt_w, pad_w], axis=1).astype(BF16)

    def pad_heads(v):
        return jnp.pad(v, ((0, 0), (0, LANES - SSD_HEADS)))[:, None, :]

    prm = dict(
        ffn1_norm=ffn1_norm[:, None], ffn1_w13=ffn1_w13.astype(BF16), ffn1_w2=ffn1_w2.astype(BF16),
        mix_norm=mix_norm[:, None], conv_w=conv_w, conv_b=conv_b[:, None],
        dt_bias=pad_heads(dt_bias), a_log=pad_heads(a_log),
        d_skip=jnp.repeat(d_skip, SSD_HEAD_DIM, axis=1)[:, None], ssd_norm=ssd_norm[:, None],
        w_ssd_out=w_ssd_out.astype(BF16), w_attn_out=w_attn_out.astype(BF16), w_o=w_o.astype(BF16),
        ffn2_norm=ffn2_norm[:, None], ffn2_w13=ffn2_w13.astype(BF16), ffn2_w2=ffn2_w2.astype(BF16))

    mats_p = _ssd_mats(SSD_CHUNK, SSD_CHUNK)
    rows_s = 8 * SEG
    mats_s = _ssd_mats(rows_s, SEG)
    bias_p = [_prompt_bias(rel_bias, gi) for gi in range(N_DIL_GROUPS)]
    caches = (cache_kv_w128, cache_kv_w512, cache_kv_w2048)
    bias_s, caches_t = [], []
    for gi in range(N_DIL_GROUPS):
        lb = caches[gi].shape[2]
        bias_s.append(_sample_bias(rel_bias, gi, jnp.arange(lb) - lb))
        caches_t.append(_kv_rows_minor(caches[gi]))
    conv_state_rows = jnp.pad(state_conv, ((0, 0), (0, 0), (0, SEG - (CONV_WIDTH - 1)), (0, 0))).reshape(
        DEPTH, n_s, CONV_DIM)
    sample_bt = tuple(min(bt, dec_batch) for bt in (32, 8, 2))

    xp = x_prompt.reshape(n_p, D_MODEL)
    xs = jnp.pad(x_sample, ((0, 0), (SEG_T0, SEG - SEG_T0 - dec_seq), (0, 0))).reshape(n_s, D_MODEL)
    ssm_p = jnp.zeros((DEPTH, batch, SSD_HEADS, SSD_HEAD_DIM, D_STATE), F32)
    ssm_s = jnp.zeros((DEPTH, dec_batch, SSD_HEADS, SSD_HEAD_DIM, D_STATE), F32)
    kvt_p = [jnp.zeros((DEPTH, batch, KV_WIDTH, min(WINDOWS[gi], seq)), F32) for gi in range(N_DIL_GROUPS)]

    kv_s = [[], [], []]
    conv_p, conv_s = [], []
    for l in range(DEPTH):
        xp = _ffn(xp, l, prm["ffn1_norm"], prm["ffn1_w13"], prm["ffn1_w2"], tm_p)
        xs = _ffn(xs, l, prm["ffn1_norm"], prm["ffn1_w13"], prm["ffn1_w2"], tm_s)
        proj_p = _inproj(xp, l, prm["mix_norm"], w_in_t, tmi_p)
        proj_s = _inproj(xs, l, prm["mix_norm"], w_in_t, tmi_s)

        y_p, ssm_p = _ssd(proj_p, l, prm, mats_p, ssm_p, batch, seq // SSD_CHUNK, SSD_CHUNK, SSD_CHUNK,
                          0, SSD_CHUNK)
        y_s, ssm_s = _ssd(proj_s, l, prm, mats_s, ssm_s, n_s // rows_s, 1, rows_s, SEG, tok_lo, tok_hi,
                          conv_state=conv_state_rows, ssm_state=state_ssm)
        attn_p = []
        for gi in range(N_DIL_GROUPS):
            o, lse, kvt_p[gi] = _attn_prompt(proj_p, l, bias_p[gi], kvt_p[gi], gi, batch, seq)
            attn_p.append((o, lse))
        attn_s = [_attn_sample(proj_s, l, caches_t[gi], bias_s[gi][0], bias_s[gi][1], gi, sample_bt[gi])
                  for gi in range(N_DIL_GROUPS)]

        xp = _mix_out(xp, y_p, proj_p, attn_p, l, prm, tm_p)
        xs = _mix_out(xs, y_s, proj_s, attn_s, l, prm, tm_s, seg_len=SEG)
        xp = _ffn(xp, l, prm["ffn2_norm"], prm["ffn2_w13"], prm["ffn2_w2"], tm_p)
        xs = _ffn(xs, l, prm["ffn2_norm"], prm["ffn2_w13"], prm["ffn2_w2"], tm_s)

        pp = proj_p.reshape(batch, seq, PROJ_W)
        ps = proj_s.reshape(dec_batch, SEG, PROJ_W)
        conv_p.append(pp[:, seq - (CONV_WIDTH - 1):, COL_XBC:COL_XBC + CONV_DIM])
        conv_s.append(ps[:, tok_hi - (CONV_WIDTH - 1):tok_hi, COL_XBC:COL_XBC + CONV_DIM])
        for gi in range(N_DIL_GROUPS):
            c0 = COL_QKV + gi * ATTN_WIDTH + ATTN_GROUP_WIDTH
            kv_s[gi].append(ps[:, tok_lo:tok_hi, c0:c0 + KV_WIDTH].reshape(
                dec_batch, dec_seq, 2, HEADS_PER_GROUP, ATTN_HEAD_DIM))

    def kv_prompt(kvt):
        keep = kvt.shape[-1]
        return kvt.reshape(DEPTH, batch, 2, HEADS_PER_GROUP, ATTN_HEAD_DIM, keep).transpose(0, 1, 5, 2, 3, 4)

    fn = final_norm[None]
    y_prompt = _final_norm(xp, fn, tm_p).reshape(batch, seq, D_MODEL)
    y_sample = _final_norm(xs, fn, tm_s).reshape(dec_batch, SEG, D_MODEL)[:, tok_lo:tok_hi]
    return (y_prompt, y_sample,
            kv_prompt(kvt_p[0]), kv_prompt(kvt_p[1]), kv_prompt(kvt_p[2]), jnp.stack(conv_p), ssm_p,
            jnp.stack(kv_s[0]), jnp.stack(kv_s[1]), jnp.stack(kv_s[2]), jnp.stack(conv_s), ssm_s)
```

```python
import functools
import math

import jax
import jax.numpy as jnp
from jax import lax
from jax.experimental import pallas as pl
from jax.experimental.pallas import tpu as pltpu

F32 = jnp.float32
BF16 = jnp.bfloat16

D_MODEL = 1024
DEPTH = 4
D_INNER = 2048
SSD_HEAD_DIM = 64
SSD_HEADS = 32
SSD_GROUPS = 8
SSD_HPG = 4
D_STATE = 128
CONV_WIDTH = 4
CONV_DIM = 4096
SSD_CHUNK = 128
ATTN_HEAD_DIM = 64
HEADS_PER_GROUP = 4
WINDOWS = (128, 512, 2048)
DILATIONS = (1, 4, 16)
N_DIL_GROUPS = 3
ATTN_WIDTH = 768
ATTN_GROUP_WIDTH = HEADS_PER_GROUP * ATTN_HEAD_DIM
KV_WIDTH = 2 * ATTN_GROUP_WIDTH
ATTN_BLOCK = 128
ATTN_SCALE = ATTN_HEAD_DIM ** -0.5
REL_BUCKETS = 32
REL_MAX_DIST = 2048
D_FF = 2816
FFN_RES = 0.5
NORM_EPS = 1e-6

LANES = 128
SEG = 8
SEG_T0 = CONV_WIDTH - 1

COL_XBC = 0
COL_Z = COL_XBC + CONV_DIM
COL_GS = COL_Z + D_INNER
COL_GA = COL_GS + D_MODEL
COL_QKV = COL_GA + D_MODEL
COL_DT = COL_QKV + 3 * ATTN_WIDTH
PROJ_W = 10752
PROJ_TN = 768

VMEM_LIMIT = 56 * 1024 * 1024


def _cparams(sem):
    return pltpu.CompilerParams(dimension_semantics=sem, vmem_limit_bytes=VMEM_LIMIT)


def _layer_spec(l, shape, single=False):
    nd = len(shape)
    kw = dict(pipeline_mode=pl.Buffered(1)) if single else {}
    return pl.BlockSpec((None,) + tuple(shape), lambda *_: (l,) + (0,) * nd, **kw)


def _const_spec(shape):
    nd = len(shape)
    return pl.BlockSpec(tuple(shape), lambda *_: (0,) * nd)


_ANY = pl.BlockSpec(memory_space=pl.ANY)


def _rms(x, g):
    return x * lax.rsqrt(jnp.mean(x * x, axis=-1, keepdims=True) + NORM_EPS) * g


def _silu(x):
    return x * jax.nn.sigmoid(x)


def _dot(a, b):
    return jnp.dot(a, b, preferred_element_type=F32)


def _dot_nt(a, b):
    return lax.dot_general(a, b, (((1,), (1,)), ((), ())), preferred_element_type=F32)


def _dot_tn(a, b):
    return lax.dot_general(a, b, (((0,), (0,)), ((), ())), preferred_element_type=F32)


def _ffn_kernel(x_ref, g_ref, w13_ref, w2_ref, *rest, n_chunks):
    o_ref = rest[-1]
    x = x_ref[...]
    xn = _rms(x, g_ref[...]).astype(BF16)
    tf = D_FF // n_chunks
    acc = None
    for c in range(n_chunks):
        a = _dot(xn, w13_ref[:, c * tf:(c + 1) * tf])
        b = _dot(xn, w13_ref[:, D_FF + c * tf:D_FF + (c + 1) * tf])
        h = (_silu(a) * b).astype(BF16)
        d = _dot(h, w2_ref[c * tf:(c + 1) * tf, :])
        acc = d if acc is None else acc + d
    y = x + FFN_RES * acc
    if len(rest) == 2:
        y = _rms(y, rest[0][...])
    o_ref[...] = y


def _ffn(x, l, g, w13, w2, tm, final_gain=None):
    n = x.shape[0]
    extra = [] if final_gain is None else [final_gain]
    return pl.pallas_call(
        functools.partial(_ffn_kernel, n_chunks=2),
        grid=(n // tm,),
        in_specs=[pl.BlockSpec((tm, D_MODEL), lambda i: (i, 0)),
                  _layer_spec(l, (1, D_MODEL)),
                  _layer_spec(l, (D_MODEL, 2 * D_FF), single=True),
                  _layer_spec(l, (D_FF, D_MODEL), single=True)] + [_const_spec((1, D_MODEL))] * len(extra),
        out_specs=pl.BlockSpec((tm, D_MODEL), lambda i: (i, 0)),
        out_shape=jax.ShapeDtypeStruct((n, D_MODEL), F32),
        compiler_params=_cparams(("parallel",)),
        name="ffn",
    )(x, g, w13, w2, *extra)


def _inproj_kernel(x_ref, g_ref, w_ref, o_ref, xn_ref):
    @pl.when(pl.program_id(1) == 0)
    def _():
        xn_ref[...] = _rms(x_ref[...], g_ref[...]).astype(BF16)

    o_ref[...] = _dot_nt(xn_ref[...], w_ref[...])


def _inproj(x, l, g, w_t, tm):
    n = x.shape[0]
    return pl.pallas_call(
        _inproj_kernel,
        grid=(n // tm, PROJ_W // PROJ_TN),
        in_specs=[pl.BlockSpec((tm, D_MODEL), lambda i, j: (i, 0)),
                  _layer_spec(l, (1, D_MODEL)),
                  pl.BlockSpec((None, PROJ_TN, D_MODEL), lambda i, j: (l, j, 0))],
        out_specs=pl.BlockSpec((tm, PROJ_TN), lambda i, j: (i, j)),
        out_shape=jax.ShapeDtypeStruct((n, PROJ_W), F32),
        scratch_shapes=[pltpu.VMEM((tm, D_MODEL), BF16)],
        compiler_params=_cparams(("parallel", "arbitrary")),
        name="inproj",
    )(x, g, w_t)


def _ssd_kernel(*refs, rows, seg_len, has_state, has_prev, token_lo, token_hi):
    (xbc_ref, z_ref, dt_ref, cw_ref, cb_ref, dtb_ref, alog_ref, dskip_ref, norm_ref,
     csm_ref, totm_ref, mask_ref) = refs[:12]
    rest = refs[12:]
    if has_state:
        cst_ref, st_in_ref = rest[:2]
        rest = rest[2:]
    if has_prev:
        rest = rest[1:]
    y_ref, st_ref, xpad_ref, conv_ref, e0_ref, e1_ref, e2_ref, xdt_ref, w_ref = rest
    nseg = rows // seg_len
    c = pl.program_id(1)

    @pl.when(c == 0)
    def _():
        xpad_ref[0:8, :] = jnp.zeros((8, CONV_DIM), F32)
        if has_state:
            st_ref[...] = st_in_ref[...]
        else:
            st_ref[...] = jnp.zeros(st_ref.shape, F32)

    row_in_seg = lax.broadcasted_iota(jnp.int32, (rows, 1), 0) % seg_len
    xbc = xbc_ref[...]
    if has_state:
        xbc = jnp.where(row_in_seg < SEG_T0, cst_ref[...], xbc)
    xpad_ref[8:8 + rows, :] = xbc
    conv = cb_ref[...] + xbc * cw_ref[CONV_WIDTH - 1:CONV_WIDTH, :]
    for i in range(CONV_WIDTH - 1):
        conv = conv + xpad_ref[5 + i:5 + i + rows, :] * cw_ref[i:i + 1, :]
    xpad_ref[0:8, :] = xbc[rows - 8:rows, :]
    conv_ref[...] = _silu(conv)

    is_token = (row_in_seg >= token_lo) & (row_in_seg < token_hi)
    dtr = dt_ref[...] + dtb_ref[...]
    dt = jnp.maximum(dtr, 0.0) + jnp.log1p(jnp.exp(-jnp.abs(dtr)))
    dt = jnp.where(is_token, dt, 0.0)
    a = dt * (-jnp.exp(alog_ref[...]))
    a_cs = jnp.dot(csm_ref[...], a, preferred_element_type=F32, precision=lax.Precision.HIGHEST)
    a_tot = jnp.dot(totm_ref[...], a, preferred_element_type=F32, precision=lax.Precision.HIGHEST)
    if rows < LANES:
        a_cs_sq = jnp.concatenate([a_cs, jnp.zeros((LANES - rows, LANES), F32)], axis=0)
    else:
        a_cs_sq = a_cs
    a_cs_t = a_cs_sq.T
    cdec = jnp.exp(a_tot)
    mask = mask_ref[...] > 0.5

    lo_lanes = lax.broadcasted_iota(jnp.int32, (rows, LANES), 1) < SSD_HEAD_DIM

    def expand(v, ref):
        for hp in range(SSD_HEADS // 2):
            b0 = jnp.broadcast_to(v[:, 2 * hp:2 * hp + 1], (rows, LANES))
            b1 = jnp.broadcast_to(v[:, 2 * hp + 1:2 * hp + 2], (rows, LANES))
            ref[:, hp * LANES:(hp + 1) * LANES] = jnp.where(lo_lanes, b0, b1)

    expand(dt, e0_ref)
    expand(dt * jnp.exp(a_tot - a_cs), e1_ref)
    expand(jnp.exp(a_cs), e2_ref)
    xs = conv_ref[:, 0:D_INNER]
    xdt_ref[...] = (xs * e0_ref[...]).astype(BF16)
    w_ref[...] = (xs * e1_ref[...]).astype(BF16)

    gw = SSD_HPG * SSD_HEAD_DIM
    head_of_lane = lax.broadcasted_iota(jnp.int32, (rows, gw), 1) // SSD_HEAD_DIM
    for g in range(SSD_GROUPS):
        sl = slice(g * gw, (g + 1) * gw)
        bg = conv_ref[:, D_INNER + g * D_STATE:D_INNER + (g + 1) * D_STATE].astype(BF16)
        cg = conv_ref[:, D_INNER + SSD_GROUPS * D_STATE + g * D_STATE:
                      D_INNER + SSD_GROUPS * D_STATE + (g + 1) * D_STATE].astype(BF16)
        cbm = _dot_nt(cg, bg)
        xd_g = xdt_ref[:, sl]
        m_parts, bd_parts = [], []
        for r in range(SSD_HPG):
            h = g * SSD_HPG + r
            seg = a_cs[:, h:h + 1] - a_cs_t[h:h + 1, 0:rows]
            decay = jnp.exp(jnp.where(mask, seg, -jnp.inf))
            m_parts.append((cbm * decay).astype(BF16))
            bd_parts.append(jnp.where(head_of_lane == r, xd_g, jnp.zeros_like(xd_g)))
        y = _dot(jnp.concatenate(m_parts, axis=1), jnp.concatenate(bd_parts, axis=0))
        yo_parts = []
        for j in range(nseg):
            r0 = j * seg_len
            s_old = st_ref[j, g * SSD_HPG:(g + 1) * SSD_HPG].reshape(gw, D_STATE)
            yo_parts.append(_dot_nt(cg[r0:r0 + seg_len], s_old.astype(BF16)))
            dec = jnp.concatenate(
                [jnp.broadcast_to(cdec[r0:r0 + 1, g * SSD_HPG + r:g * SSD_HPG + r + 1], (SSD_HEAD_DIM, D_STATE))
                 for r in range(SSD_HPG)], axis=0)
            s_new = s_old * dec + _dot_tn(w_ref[r0:r0 + seg_len, sl], bg[r0:r0 + seg_len])
            st_ref[j, g * SSD_HPG:(g + 1) * SSD_HPG] = s_new.reshape(SSD_HPG, SSD_HEAD_DIM, D_STATE)
        yo = yo_parts[0] if nseg == 1 else jnp.concatenate(yo_parts, axis=0)
        y = y + yo * e2_ref[:, sl]
        y = y + dskip_ref[:, sl] * conv_ref[:, sl]
        y = y * _silu(z_ref[:, sl])
        y = y * lax.rsqrt(jnp.mean(y * y, axis=-1, keepdims=True) + NORM_EPS) * norm_ref[:, sl]
        y_ref[:, sl] = y.astype(BF16)


def _ssd(proj, l, prm, mats, st_shape, st_prev, n_outer, n_inner, rows, seg_len, token_lo, token_hi,
         conv_state=None, ssm_state=None):
    has_state = ssm_state is not None
    nseg = rows // seg_len
    n = proj.shape[0]
    st_block = (None, nseg, SSD_HEADS, SSD_HEAD_DIM, D_STATE)

    def rowmap(cb):
        return lambda o, c: (o * n_inner + c, cb)

    def st_map(o, c):
        return (l, o, 0, 0, 0)

    in_specs = [pl.BlockSpec((rows, CONV_DIM), rowmap(COL_XBC // CONV_DIM)),
                pl.BlockSpec((rows, D_INNER), rowmap(COL_Z // D_INNER)),
                pl.BlockSpec((rows, LANES), rowmap(COL_DT // LANES)),
                _layer_spec(l, (CONV_WIDTH, CONV_DIM)), _layer_spec(l, (1, CONV_DIM)),
                _layer_spec(l, (1, LANES)), _layer_spec(l, (1, LANES)),
                _layer_spec(l, (1, D_INNER)), _layer_spec(l, (1, D_INNER)),
                _const_spec((rows, rows)), _const_spec((rows, rows)), _const_spec((rows, rows))]
    args = [proj, proj, proj, prm["conv_w"], prm["conv_b"], prm["dt_bias"], prm["a_log"],
            prm["d_skip"], prm["ssd_norm"], mats[0], mats[1], mats[2]]
    if has_state:
        in_specs += [pl.BlockSpec((None, rows, CONV_DIM), lambda o, c: (l, o * n_inner + c, 0)),
                     pl.BlockSpec(st_block, st_map)]
        args += [conv_state, ssm_state]
    aliases = {}
    if st_prev is not None:
        in_specs.append(_ANY)
        args.append(st_prev)
        aliases = {len(args) - 1: 1}
    y, st = pl.pallas_call(
        functools.partial(_ssd_kernel, rows=rows, seg_len=seg_len, has_state=has_state,
                          has_prev=st_prev is not None, token_lo=token_lo, token_hi=token_hi),
        grid=(n_outer, n_inner),
        in_specs=in_specs,
        out_specs=[pl.BlockSpec((rows, D_INNER), rowmap(0)), pl.BlockSpec(st_block, st_map)],
        out_shape=[jax.ShapeDtypeStruct((n, D_INNER), BF16),
                   jax.ShapeDtypeStruct(st_shape, F32)],
        scratch_shapes=[pltpu.VMEM((8 + rows, CONV_DIM), F32),
                        pltpu.VMEM((rows, CONV_DIM), F32),
                        pltpu.VMEM((rows, D_INNER), F32),
                        pltpu.VMEM((rows, D_INNER), F32),
                        pltpu.VMEM((rows, D_INNER), F32),
                        pltpu.VMEM((rows, D_INNER), BF16),
                        pltpu.VMEM((rows, D_INNER), BF16)],
        input_output_aliases=aliases,
        compiler_params=_cparams(("parallel", "arbitrary")),
        name="ssd_state" if has_state else "ssd_prompt",
    )(*args)
    return y, st


def _attn_prompt_kernel(q0, q1, k0, k1, v0, v1, bias_ref, *rest, dil, seq, keep):
    o_ref, lse_ref, kvt_ref = rest[-3:]
    nblk = seq // (ATTN_BLOCK * dil)
    lo_lanes = lax.broadcasted_iota(jnp.int32, (ATTN_BLOCK, LANES), 1) < ATTN_HEAD_DIM

    tc = min(keep, 512)
    for half, (k, v) in enumerate(((k0, v0), (k1, v1))):
        for c0 in range(0, keep, tc):
            src = pl.ds(seq - keep + c0, tc)
            kvt_ref[half * LANES:(half + 1) * LANES, c0:c0 + tc] = k[src, :].T
            kvt_ref[ATTN_GROUP_WIDTH + half * LANES:ATTN_GROUP_WIDTH + (half + 1) * LANES, c0:c0 + tc] = v[src, :].T

    def rows(start):
        if dil == 1:
            return pl.ds(pl.multiple_of(start, ATTN_BLOCK), ATTN_BLOCK)
        return pl.ds(start, ATTN_BLOCK, stride=dil)

    def body(i, carry):
        r = i // nblk
        blk = i % nblk
        cur = rows(r + blk * ATTN_BLOCK * dil)
        prev = rows(r + jnp.maximum(blk - 1, 0) * ATTN_BLOCK * dil)
        tab = jnp.minimum(blk, 1)
        for half, (q, k, v) in enumerate(((q0, k0, v0), (q1, k1, v1))):
            qb = q[cur, :]
            kk = jnp.concatenate([k[prev, :], k[cur, :]], axis=0).astype(BF16)
            vv = jnp.concatenate([v[prev, :], v[cur, :]], axis=0).astype(BF16)
            outs, lses = [], []
            for hh in range(2):
                lanes = lo_lanes if hh == 0 else jnp.logical_not(lo_lanes)
                qm = jnp.where(lanes, qb, 0.0).astype(BF16)
                s = _dot_nt(qm, kk) * ATTN_SCALE + bias_ref[tab, 2 * half + hh]
                m = jnp.max(s, axis=-1, keepdims=True)
                p = jnp.exp(s - m)
                l = jnp.sum(p, axis=-1, keepdims=True)
                outs.append(_dot(p.astype(BF16), vv) / l)
                lses.append(jnp.broadcast_to(m + jnp.log(l), (ATTN_BLOCK, LANES)))
            o_ref[half, cur, :] = jnp.where(lo_lanes, outs[0], outs[1])
            lse_ref[half, cur, :] = jnp.where(lo_lanes, lses[0], lses[1])
        return carry

    lax.fori_loop(0, dil * nblk, body, 0, unroll=4)


def _attn_prompt(proj, l, bias, kvt_shape, kvt_prev, gi, batch, seq):
    dil = DILATIONS[gi]
    keep = kvt_shape[-1]
    base = (COL_QKV + gi * ATTN_WIDTH) // LANES

    def col(j):
        return pl.BlockSpec((seq, LANES), lambda b: (b, base + j))

    out_spec = pl.BlockSpec((2, seq, LANES), lambda b: (0, b, 0))
    shape = jax.ShapeDtypeStruct((2, batch * seq, LANES), F32)
    prev = [] if kvt_prev is None else [kvt_prev]
    return pl.pallas_call(
        functools.partial(_attn_prompt_kernel, dil=dil, seq=seq, keep=keep),
        grid=(batch,),
        in_specs=[col(0), col(1), col(2), col(3), col(4), col(5),
                  _const_spec((2, HEADS_PER_GROUP, ATTN_BLOCK, 2 * ATTN_BLOCK))] + ([_ANY] if prev else []),
        out_specs=[out_spec, out_spec,
                   pl.BlockSpec((None, None, KV_WIDTH, keep), lambda b: (l, b, 0, 0))],
        out_shape=[shape, shape, jax.ShapeDtypeStruct(kvt_shape, F32)],
        input_output_aliases={7: 2} if prev else {},
        compiler_params=_cparams(("parallel",)),
        name=f"attn_prompt_d{dil}",
    )(proj, proj, proj, proj, proj, proj, bias, *prev)


def _attn_sample_kernel(q0, q1, kn0, kn1, vn0, vn1, cache_ref, bias_c_ref, bias_n_ref, o_ref, lse_ref, *, bt):
    lo8 = lax.broadcasted_iota(jnp.int32, (SEG, LANES), 1) < ATTN_HEAD_DIM
    lo16 = lax.broadcasted_iota(jnp.int32, (2 * SEG, LANES), 1) < ATTN_HEAD_DIM
    top16 = lax.broadcasted_iota(jnp.int32, (2 * SEG, LANES), 0) < SEG
    head_lanes = lo16 == top16

    def body(b, carry):
        seg = pl.ds(pl.multiple_of(b * SEG, SEG), SEG)
        for half, (q, kn, vn) in enumerate(((q0, kn0, vn0), (q1, kn1, vn1))):
            qb = q[seg, :]
            q16 = jnp.where(head_lanes, jnp.concatenate([qb, qb], axis=0), 0.0).astype(BF16)
            knew = kn[seg, :].astype(BF16)
            vnew = vn[seg, :].astype(BF16)
            kt = cache_ref[b, half * LANES:(half + 1) * LANES, :].astype(BF16)
            vt = cache_ref[b, ATTN_GROUP_WIDTH + half * LANES:
                           ATTN_GROUP_WIDTH + (half + 1) * LANES, :].astype(BF16)
            sc = _dot(q16, kt) * ATTN_SCALE + bias_c_ref[half]
            sn = _dot_nt(q16, knew) * ATTN_SCALE + bias_n_ref[half]
            m = jnp.maximum(jnp.max(sc, axis=-1, keepdims=True), jnp.max(sn, axis=-1, keepdims=True))
            pc = jnp.exp(sc - m)
            pn = jnp.exp(sn - m)
            l = jnp.sum(pc, axis=-1, keepdims=True) + jnp.sum(pn, axis=-1, keepdims=True)
            o = (_dot_nt(pc.astype(BF16), vt) + _dot(pn.astype(BF16), vnew)) / l
            lse = jnp.broadcast_to(m + jnp.log(l), (2 * SEG, LANES))
            o_ref[half, seg, :] = jnp.where(lo8, o[0:SEG], o[SEG:2 * SEG])
            lse_ref[half, seg, :] = jnp.where(lo8, lse[0:SEG], lse[SEG:2 * SEG])
        return carry

    lax.fori_loop(0, bt, body, 0, unroll=min(bt, 8))


def _attn_sample(proj, l, cache_t, bias_c, bias_n, gi, bt):
    nb = cache_t.shape[1]
    rows_c = cache_t.shape[3]
    base = (COL_QKV + gi * ATTN_WIDTH) // LANES

    def col(j):
        return pl.BlockSpec((bt * SEG, LANES), lambda i: (i, base + j))

    out_spec = pl.BlockSpec((2, bt * SEG, LANES), lambda i: (0, i, 0))
    shape = jax.ShapeDtypeStruct((2, nb * SEG, LANES), F32)
    return pl.pallas_call(
        functools.partial(_attn_sample_kernel, bt=bt),
        grid=(nb // bt,),
        in_specs=[col(0), col(1), col(2), col(3), col(4), col(5),
                  pl.BlockSpec((None, bt, KV_WIDTH, rows_c), lambda i: (l, i, 0, 0)),
                  _const_spec((2, 2 * SEG, rows_c)),
                  _const_spec((2, 2 * SEG, SEG))],
        out_specs=[out_spec, out_spec],
        out_shape=[shape, shape],
        compiler_params=_cparams(("parallel",)),
        name=f"attn_sample_d{DILATIONS[gi]}",
    )(proj, proj, proj, proj, proj, proj, cache_t, bias_c, bias_n)


def _mix_out_kernel(x_ref, y_ref, gs_ref, ga_ref, o0, o1, o2, l0, l1, l2, wso_ref, wao_ref, wo_ref, out_ref,
                    *, token_lo, token_hi, seg_len):
    ssd_out = _dot(y_ref[...], wso_ref[...])
    attn_out = None
    for half in range(2):
        ls = [l0[half], l1[half], l2[half]]
        os_ = [o0[half], o1[half], o2[half]]
        m = jnp.maximum(jnp.maximum(ls[0], ls[1]), ls[2])
        es = [jnp.exp(l - m) for l in ls]
        den = es[0] + es[1] + es[2]
        o = (es[0] / den) * os_[0] + (es[1] / den) * os_[1] + (es[2] / den) * os_[2]
        d = _dot(o.astype(BF16), wao_ref[half * LANES:(half + 1) * LANES, :])
        attn_out = d if attn_out is None else attn_out + d
    merged = jax.nn.sigmoid(gs_ref[...]) * ssd_out + jax.nn.sigmoid(ga_ref[...]) * attn_out
    upd = _dot(merged.astype(BF16), wo_ref[...])
    if seg_len is not None:
        rows = x_ref.shape[0]
        ris = lax.broadcasted_iota(jnp.int32, (rows, 1), 0) % seg_len
        upd = jnp.where((ris >= token_lo) & (ris < token_hi), upd, 0.0)
    out_ref[...] = x_ref[...] + upd


def _mix_out(x, y, proj, attn, l, prm, tm, seg_len=None):
    n = x.shape[0]
    half_spec = pl.BlockSpec((2, tm, LANES), lambda i: (0, i, 0))
    (o0, l0), (o1, l1), (o2, l2) = attn
    return pl.pallas_call(
        functools.partial(_mix_out_kernel, token_lo=SEG_T0, token_hi=SEG_T0 + 4, seg_len=seg_len),
        grid=(n // tm,),
        in_specs=[pl.BlockSpec((tm, D_MODEL), lambda i: (i, 0)),
                  pl.BlockSpec((tm, D_INNER), lambda i: (i, 0)),
                  pl.BlockSpec((tm, D_MODEL), lambda i: (i, COL_GS // D_MODEL)),
                  pl.BlockSpec((tm, D_MODEL), lambda i: (i, COL_GA // D_MODEL)),
                  half_spec, half_spec, half_spec, half_spec, half_spec, half_spec,
                  _layer_spec(l, (D_INNER, D_MODEL), single=True),
                  _layer_spec(l, (ATTN_GROUP_WIDTH, D_MODEL), single=True),
                  _layer_spec(l, (D_MODEL, D_MODEL), single=True)],
        out_specs=pl.BlockSpec((tm, D_MODEL), lambda i: (i, 0)),
        out_shape=jax.ShapeDtypeStruct((n, D_MODEL), F32),
        compiler_params=_cparams(("parallel",)),
        name="mix_out",
    )(x, y, proj, proj, o0, o1, o2, l0, l1, l2, prm["w_ssd_out"], prm["w_attn_out"], prm["w_o"])


def _rel_bucket(dist):
    max_exact = REL_BUCKETS // 2
    d = jnp.maximum(dist, 1).astype(F32)
    large = max_exact + (jnp.log(d / max_exact) / math.log(REL_MAX_DIST / max_exact)
                         * (REL_BUCKETS - max_exact)).astype(jnp.int32)
    large = jnp.minimum(large, REL_BUCKETS - 1)
    return jnp.where(dist < max_exact, dist, large)


def _bias_lookup(tab, dist):
    onehot = (_rel_bucket(dist)[..., None] == jnp.arange(REL_BUCKETS)).astype(F32)
    return jnp.einsum("...k,kh->h...", onehot, tab.astype(F32), precision=lax.Precision.HIGHEST)


def _prompt_bias(rel_bias, gi):
    dil, reach = DILATIONS[gi], WINDOWS[gi] // DILATIONS[gi]
    blk = ATTN_BLOCK
    step = (jnp.arange(blk)[:, None] + blk) - jnp.arange(2 * blk)[None, :]
    tab = rel_bias[:, gi * HEADS_PER_GROUP:(gi + 1) * HEADS_PER_GROUP]
    bias = _bias_lookup(tab, jnp.maximum(step, 0) * dil)
    valid = (step >= 0) & (step <= reach)
    first = valid & (jnp.arange(2 * blk)[None, :] >= blk)
    neg = jnp.float32(-jnp.inf)
    return jnp.stack([jnp.where(first[None], bias, neg), jnp.where(valid[None], bias, neg)])


def _sample_bias(rel_bias, gi, row_pos):
    dil, window = DILATIONS[gi], WINDOWS[gi]
    tab = rel_bias[:, gi * HEADS_PER_GROUP:(gi + 1) * HEADS_PER_GROUP]
    seg_row = jnp.arange(SEG)
    is_tok = (seg_row >= SEG_T0) & (seg_row < SEG_T0 + 4)
    t = jnp.where(is_tok, seg_row - SEG_T0, 0)
    neg = jnp.float32(-jnp.inf)

    def table(dist, ok):
        ok = ok & (dist >= 0) & (dist % dil == 0) & (dist // dil <= window // dil)
        tbl = jnp.where(ok[None], _bias_lookup(tab, jnp.maximum(dist, 0)), neg)
        return tbl.reshape(2, 2 * SEG, dist.shape[1])

    dist_c = t[:, None] - row_pos[None, :]
    bias_c = table(dist_c, jnp.ones(dist_c.shape, bool))
    key_tok = seg_row - SEG_T0
    dist_n = t[:, None] - key_tok[None, :]
    bias_n = table(dist_n, jnp.broadcast_to(is_tok[None, :], dist_n.shape))
    return bias_c, bias_n


def _ssd_mats(rows, seg_len):
    l = jnp.arange(rows)[:, None]
    s = jnp.arange(rows)[None, :]
    same = (l // seg_len) == (s // seg_len)
    causal = same & (s <= l)
    return causal.astype(F32), same.astype(F32), causal.astype(F32)


def _kv_rows_minor(c):
    d, b, rows = c.shape[:3]
    return c.transpose(0, 1, 3, 4, 5, 2).reshape(d, b, KV_WIDTH, rows)


def kernel(x_prompt, x_sample, cache_kv_w128, cache_kv_w512, cache_kv_w2048, state_conv, state_ssm, rel_bias,
           ffn1_norm, ffn1_w13, ffn1_w2, mix_norm, w_in, conv_w, conv_b, dt_bias, a_log, d_skip, ssd_norm,
           w_ssd_out, w_attn_out, w_o, ffn2_norm, ffn2_w13, ffn2_w2, final_norm):
    batch, seq, _ = x_prompt.shape
    dec_batch, dec_seq, _ = x_sample.shape
    n_p = batch * seq
    n_s = dec_batch * SEG
    tm_p, tm_s = 512, min(512, n_s)
    tmi_p, tmi_s = 2048, min(1024, n_s)
    tok_lo, tok_hi = SEG_T0, SEG_T0 + dec_seq

    w_t = jnp.swapaxes(w_in, 1, 2)
    z_w, xbc_w, dt_w, q_w, k_w, v_w, gs_w, ga_w = jnp.split(
        w_t, [2048, 6144, 6176, 6944, 7712, 8480, 9504], axis=1)
    qkv_w = []
    for gi in range(N_DIL_GROUPS):
        sl = slice(gi * ATTN_GROUP_WIDTH, (gi + 1) * ATTN_GROUP_WIDTH)
        qkv_w += [q_w[:, sl], k_w[:, sl], v_w[:, sl]]
    pad_w = jnp.zeros((DEPTH, PROJ_W - COL_DT - SSD_HEADS, D_MODEL), w_in.dtype)
    w_in_t = jnp.concatenate([xbc_w, z_w, gs_w, ga_w] + qkv_w + [dt_w, pad_w], axis=1).astype(BF16)

    def pad_heads(v):
        return jnp.pad(v, ((0, 0), (0, LANES - SSD_HEADS)))[:, None, :]

    prm = dict(
        ffn1_norm=ffn1_norm[:, None], ffn1_w13=ffn1_w13.astype(BF16), ffn1_w2=ffn1_w2.astype(BF16),
        mix_norm=mix_norm[:, None], conv_w=conv_w, conv_b=conv_b[:, None],
        dt_bias=pad_heads(dt_bias), a_log=pad_heads(a_log),
        d_skip=jnp.repeat(d_skip, SSD_HEAD_DIM, axis=1)[:, None], ssd_norm=ssd_norm[:, None],
        w_ssd_out=w_ssd_out.astype(BF16), w_attn_out=w_attn_out.astype(BF16), w_o=w_o.astype(BF16),
        ffn2_norm=ffn2_norm[:, None], ffn2_w13=ffn2_w13.astype(BF16), ffn2_w2=ffn2_w2.astype(BF16))

    mats_p = _ssd_mats(SSD_CHUNK, SSD_CHUNK)
    rows_s = 8 * SEG
    mats_s = _ssd_mats(rows_s, SEG)
    bias_p = [_prompt_bias(rel_bias, gi) for gi in range(N_DIL_GROUPS)]
    caches = (cache_kv_w128, cache_kv_w512, cache_kv_w2048)
    bias_s, caches_t = [], []
    for gi in range(N_DIL_GROUPS):
        lb = caches[gi].shape[2]
        bias_s.append(_sample_bias(rel_bias, gi, jnp.arange(lb) - lb))
        caches_t.append(_kv_rows_minor(caches[gi]))
    conv_state_rows = jnp.pad(state_conv, ((0, 0), (0, 0), (0, SEG - (CONV_WIDTH - 1)), (0, 0))).reshape(
        DEPTH, n_s, CONV_DIM)
    sample_bt = tuple(min(bt, dec_batch) for bt in (32, 8, 2))

    xp = x_prompt.reshape(n_p, D_MODEL)
    xs = jnp.pad(x_sample, ((0, 0), (SEG_T0, SEG - SEG_T0 - dec_seq), (0, 0))).reshape(n_s, D_MODEL)
    ssm_p_shape = (DEPTH, batch, SSD_HEADS, SSD_HEAD_DIM, D_STATE)
    ssm_s_shape = (DEPTH, dec_batch, SSD_HEADS, SSD_HEAD_DIM, D_STATE)
    kvt_shapes = [(DEPTH, batch, KV_WIDTH, min(WINDOWS[gi], seq)) for gi in range(N_DIL_GROUPS)]
    ssm_p = ssm_s = None
    kvt_p = [None] * N_DIL_GROUPS
    fn = final_norm[None]

    kv_s = [[], [], []]
    conv_p, conv_s = [], []
    for l in range(DEPTH):
        xp = _ffn(xp, l, prm["ffn1_norm"], prm["ffn1_w13"], prm["ffn1_w2"], tm_p)
        xs = _ffn(xs, l, prm["ffn1_norm"], prm["ffn1_w13"], prm["ffn1_w2"], tm_s)
        proj_p = _inproj(xp, l, prm["mix_norm"], w_in_t, tmi_p)
        proj_s = _inproj(xs, l, prm["mix_norm"], w_in_t, tmi_s)

        y_p, ssm_p = _ssd(proj_p, l, prm, mats_p, ssm_p_shape, ssm_p, batch, seq // SSD_CHUNK, SSD_CHUNK,
                          SSD_CHUNK, 0, SSD_CHUNK)
        y_s, ssm_s = _ssd(proj_s, l, prm, mats_s, ssm_s_shape, ssm_s, n_s // rows_s, 1, rows_s, SEG,
                          tok_lo, tok_hi, conv_state=conv_state_rows, ssm_state=state_ssm)
        attn_p = []
        for gi in range(N_DIL_GROUPS):
            o, lse, kvt_p[gi] = _attn_prompt(proj_p, l, bias_p[gi], kvt_shapes[gi], kvt_p[gi], gi, batch, seq)
            attn_p.append((o, lse))
        attn_s = [_attn_sample(proj_s, l, caches_t[gi], bias_s[gi][0], bias_s[gi][1], gi, sample_bt[gi])
                  for gi in range(N_DIL_GROUPS)]

        xp = _mix_out(xp, y_p, proj_p, attn_p, l, prm, tm_p)
        xs = _mix_out(xs, y_s, proj_s, attn_s, l, prm, tm_s, seg_len=SEG)
        last = fn if l == DEPTH - 1 else None
        xp = _ffn(xp, l, prm["ffn2_norm"], prm["ffn2_w13"], prm["ffn2_w2"], tm_p, final_gain=last)
        xs = _ffn(xs, l, prm["ffn2_norm"], prm["ffn2_w13"], prm["ffn2_w2"], tm_s, final_gain=last)

        pp = proj_p.reshape(batch, seq, PROJ_W)
        ps = proj_s.reshape(dec_batch, SEG, PROJ_W)
        conv_p.append(pp[:, seq - (CONV_WIDTH - 1):, COL_XBC:COL_XBC + CONV_DIM])
        conv_s.append(ps[:, tok_hi - (CONV_WIDTH - 1):tok_hi, COL_XBC:COL_XBC + CONV_DIM])
        for gi in range(N_DIL_GROUPS):
            c0 = COL_QKV + gi * ATTN_WIDTH + ATTN_GROUP_WIDTH
            kv_s[gi].append(ps[:, tok_lo:tok_hi, c0:c0 + KV_WIDTH].reshape(
                dec_batch, dec_seq, 2, HEADS_PER_GROUP, ATTN_HEAD_DIM))

    def kv_prompt(kvt):
        keep = kvt.shape[-1]
        return kvt.reshape(DEPTH, batch, 2, HEADS_PER_GROUP, ATTN_HEAD_DIM, keep).transpose(0, 1, 5, 2, 3, 4)

    y_prompt = xp.reshape(batch, seq, D_MODEL)
    y_sample = xs.reshape(dec_batch, SEG, D_MODEL)[:, tok_lo:tok_hi]
    return (y_prompt, y_sample,
            kv_prompt(kvt_p[0]), kv_prompt(kvt_p[1]), kv_prompt(kvt_p[2]), jnp.stack(conv_p), ssm_p,
            jnp.stack(kv_s[0]), jnp.stack(kv_s[1]), jnp.stack(kv_s[2]), jnp.stack(conv_s), ssm_s)
```

```python
import functools
import math

import jax
import jax.numpy as jnp
from jax import lax
from jax.experimental import pallas as pl
from jax.experimental.pallas import tpu as pltpu

F32 = jnp.float32
BF16 = jnp.bfloat16

D_MODEL = 1024
DEPTH = 4
D_INNER = 2048
SSD_HEAD_DIM = 64
SSD_HEADS = 32
SSD_GROUPS = 8
SSD_HPG = 4
D_STATE = 128
CONV_WIDTH = 4
CONV_DIM = 4096
SSD_CHUNK = 128
ATTN_HEAD_DIM = 64
HEADS_PER_GROUP = 4
WINDOWS = (128, 512, 2048)
DILATIONS = (1, 4, 16)
N_DIL_GROUPS = 3
ATTN_WIDTH = 768
ATTN_GROUP_WIDTH = HEADS_PER_GROUP * ATTN_HEAD_DIM
KV_WIDTH = 2 * ATTN_GROUP_WIDTH
ATTN_BLOCK = 128
ATTN_SCALE = ATTN_HEAD_DIM ** -0.5
REL_BUCKETS = 32
REL_MAX_DIST = 2048
D_FF = 2816
FFN_RES = 0.5
NORM_EPS = 1e-6
LOG2E = 1.4426950408889634

LANES = 128
SEG = 8
SEG_T0 = CONV_WIDTH - 1

COL_XBC = 0
COL_Z = COL_XBC + CONV_DIM
COL_GS = COL_Z + D_INNER
COL_GA = COL_GS + D_MODEL
COL_QKV = COL_GA + D_MODEL
COL_DT = COL_QKV + 3 * ATTN_WIDTH
PROJ_W = 10752
PROJ_TN = 768

VMEM_LIMIT = 56 * 1024 * 1024


def _cparams(sem):
    return pltpu.CompilerParams(dimension_semantics=sem, vmem_limit_bytes=VMEM_LIMIT)


def _layer_spec(l, shape, single=False):
    nd = len(shape)
    kw = dict(pipeline_mode=pl.Buffered(1)) if single else {}
    return pl.BlockSpec((None,) + tuple(shape), lambda *_: (l,) + (0,) * nd, **kw)


def _const_spec(shape):
    nd = len(shape)
    return pl.BlockSpec(tuple(shape), lambda *_: (0,) * nd)


_ANY = pl.BlockSpec(memory_space=pl.ANY)


def _rms(x, g):
    return x * lax.rsqrt(jnp.mean(x * x, axis=-1, keepdims=True) + NORM_EPS) * g


def _silu(x):
    return x * jax.nn.sigmoid(x)


def _dot(a, b):
    return jnp.dot(a, b, preferred_element_type=F32)


def _dot_nt(a, b):
    return lax.dot_general(a, b, (((1,), (1,)), ((), ())), preferred_element_type=F32)


def _dot_tn(a, b):
    return lax.dot_general(a, b, (((0,), (0,)), ((), ())), preferred_element_type=F32)


def _ffn_kernel(x_ref, g_ref, w13_ref, w2_ref, *rest, n_chunks):
    o_ref = rest[-1]
    x = x_ref[...]
    xn = _rms(x, g_ref[...]).astype(BF16)
    tf = D_FF // n_chunks
    acc = None
    for c in range(n_chunks):
        a = _dot(xn, w13_ref[:, c * tf:(c + 1) * tf])
        b = _dot(xn, w13_ref[:, D_FF + c * tf:D_FF + (c + 1) * tf])
        h = (_silu(a) * b).astype(BF16)
        d = _dot(h, w2_ref[c * tf:(c + 1) * tf, :])
        acc = d if acc is None else acc + d
    y = x + FFN_RES * acc
    if len(rest) == 2:
        y = _rms(y, rest[0][...])
    o_ref[...] = y


def _ffn(x, l, g, w13, w2, tm, final_gain=None):
    n = x.shape[0]
    extra = [] if final_gain is None else [final_gain]
    return pl.pallas_call(
        functools.partial(_ffn_kernel, n_chunks=2),
        grid=(n // tm,),
        in_specs=[pl.BlockSpec((tm, D_MODEL), lambda i: (i, 0)),
                  _layer_spec(l, (1, D_MODEL)),
                  _layer_spec(l, (D_MODEL, 2 * D_FF), single=True),
                  _layer_spec(l, (D_FF, D_MODEL), single=True)] + [_const_spec((1, D_MODEL))] * len(extra),
        out_specs=pl.BlockSpec((tm, D_MODEL), lambda i: (i, 0)),
        out_shape=jax.ShapeDtypeStruct((n, D_MODEL), F32),
        compiler_params=_cparams(("parallel",)),
        name="ffn",
    )(x, g, w13, w2, *extra)


def _inproj_kernel(x_ref, g_ref, w_ref, o_ref, xn_ref):
    @pl.when(pl.program_id(1) == 0)
    def _():
        xn_ref[...] = _rms(x_ref[...], g_ref[...]).astype(BF16)

    o_ref[...] = _dot_nt(xn_ref[...], w_ref[...])


def _inproj(x, l, g, w_t, tm):
    n = x.shape[0]
    return pl.pallas_call(
        _inproj_kernel,
        grid=(n // tm, PROJ_W // PROJ_TN),
        in_specs=[pl.BlockSpec((tm, D_MODEL), lambda i, j: (i, 0)),
                  _layer_spec(l, (1, D_MODEL)),
                  pl.BlockSpec((None, PROJ_TN, D_MODEL), lambda i, j: (l, j, 0))],
        out_specs=pl.BlockSpec((tm, PROJ_TN), lambda i, j: (i, j)),
        out_shape=jax.ShapeDtypeStruct((n, PROJ_W), F32),
        scratch_shapes=[pltpu.VMEM((tm, D_MODEL), BF16)],
        compiler_params=_cparams(("parallel", "arbitrary")),
        name="inproj",
    )(x, g, w_t)


def _ssd_kernel(*refs, rows, seg_len, has_state, has_prev, token_lo, token_hi):
    (xbc_ref, z_ref, dt_ref, cw_ref, cb_ref, dtb_ref, alog_ref, dskip_ref, norm_ref,
     csm_ref, totm_ref, mask_ref, spread_ref) = refs[:13]
    rest = refs[13:]
    if has_state:
        cst_ref, st_in_ref = rest[:2]
        rest = rest[2:]
    if has_prev:
        rest = rest[1:]
    y_ref, st_ref, xpad_ref, conv_ref, e0_ref, e1_ref, e2_ref, xdt_ref, w_ref = rest
    nseg = rows // seg_len
    c = pl.program_id(1)

    @pl.when(c == 0)
    def _():
        xpad_ref[:, 0:16, :] = jnp.zeros((CONV_DIM // LANES, 16, LANES), F32)
        if has_state:
            st_ref[...] = st_in_ref[...]
        else:
            st_ref[...] = jnp.zeros(st_ref.shape, F32)

    row_in_seg = lax.broadcasted_iota(jnp.int32, (rows, 1), 0) % seg_len
    xbc = xbc_ref[...]
    if has_state:
        xbc = jnp.where(row_in_seg < SEG_T0, cst_ref[...], xbc)
    for s in range(CONV_DIM // LANES):
        sl = slice(s * LANES, (s + 1) * LANES)
        xs_ = xbc[:, sl]
        xpad_ref[s, pl.ds(16, rows, stride=2), :] = xs_
        conv = cb_ref[:, sl] + xs_ * cw_ref[CONV_WIDTH - 1:CONV_WIDTH, sl]
        for i in range(CONV_WIDTH - 1):
            back = CONV_WIDTH - 1 - i
            conv = conv + xpad_ref[s, pl.ds(16 - 2 * back, rows, stride=2), :] * cw_ref[i:i + 1, sl]
        xpad_ref[s, pl.ds(0, 8, stride=2), :] = xs_[rows - 8:rows]
        conv_ref[:, sl] = _silu(conv)

    is_token = (row_in_seg >= token_lo) & (row_in_seg < token_hi)
    dtr = dt_ref[...] + dtb_ref[...]
    dt = jnp.maximum(dtr, 0.0) + jnp.log1p(jnp.exp(-jnp.abs(dtr)))
    dt = jnp.where(is_token, dt, 0.0)
    a = dt * (-jnp.exp(alog_ref[...]))
    a_cs = jnp.dot(csm_ref[...], a, preferred_element_type=F32, precision=lax.Precision.HIGHEST)
    a_tot = jnp.dot(totm_ref[...], a, preferred_element_type=F32, precision=lax.Precision.HIGHEST)
    a_l2 = a_cs * LOG2E
    if rows < LANES:
        a_l2_sq = jnp.concatenate([a_l2, jnp.zeros((LANES - rows, LANES), F32)], axis=0)
    else:
        a_l2_sq = a_l2
    a_l2_t = a_l2_sq.T
    cdec = jnp.exp(a_tot)
    mask = mask_ref[...] > 0.5

    def split3(v):
        hi = v.astype(BF16)
        r1 = v - hi.astype(F32)
        mid = r1.astype(BF16)
        lo = (r1 - mid.astype(F32)).astype(BF16)
        return jnp.concatenate([hi, mid, lo], axis=1)

    lhs = jnp.concatenate([split3(dt), split3(dt * jnp.exp(a_tot - a_cs)), split3(jnp.exp(a_cs))], axis=0)
    spread = _dot(lhs, spread_ref[...])
    e0_ref[...] = spread[0:rows]
    e1_ref[...] = spread[rows:2 * rows]
    e2_ref[...] = spread[2 * rows:3 * rows]
    xs = conv_ref[:, 0:D_INNER]
    xdt_ref[...] = (xs * e0_ref[...]).astype(BF16)
    w_ref[...] = (xs * e1_ref[...]).astype(BF16)

    gw = SSD_HPG * SSD_HEAD_DIM
    head_of_lane = lax.broadcasted_iota(jnp.int32, (rows, gw), 1) // SSD_HEAD_DIM
    for g in range(SSD_GROUPS):
        sl = slice(g * gw, (g + 1) * gw)
        bg = conv_ref[:, D_INNER + g * D_STATE:D_INNER + (g + 1) * D_STATE].astype(BF16)
        cg = conv_ref[:, D_INNER + SSD_GROUPS * D_STATE + g * D_STATE:
                      D_INNER + SSD_GROUPS * D_STATE + (g + 1) * D_STATE].astype(BF16)
        cbm = _dot_nt(cg, bg)
        xd_g = xdt_ref[:, sl]
        m_parts, bd_parts = [], []
        for r in range(SSD_HPG):
            h = g * SSD_HPG + r
            seg = a_l2[:, h:h + 1] - a_l2_t[h:h + 1, 0:rows]
            decay = jnp.exp2(jnp.where(mask, seg, -jnp.inf))
            m_parts.append((cbm * decay).astype(BF16))
            bd_parts.append(jnp.where(head_of_lane == r, xd_g, jnp.zeros_like(xd_g)))
        y = _dot(jnp.concatenate(m_parts, axis=1), jnp.concatenate(bd_parts, axis=0))
        yo_parts = []
        for j in range(nseg):
            r0 = j * seg_len
            s_old = st_ref[j, g * SSD_HPG:(g + 1) * SSD_HPG].reshape(gw, D_STATE)
            yo_parts.append(_dot_nt(cg[r0:r0 + seg_len], s_old.astype(BF16)))
            dec = jnp.concatenate(
                [jnp.broadcast_to(cdec[r0:r0 + 1, g * SSD_HPG + r:g * SSD_HPG + r + 1], (SSD_HEAD_DIM, D_STATE))
                 for r in range(SSD_HPG)], axis=0)
            s_new = s_old * dec + _dot_tn(w_ref[r0:r0 + seg_len, sl], bg[r0:r0 + seg_len])
            st_ref[j, g * SSD_HPG:(g + 1) * SSD_HPG] = s_new.reshape(SSD_HPG, SSD_HEAD_DIM, D_STATE)
        yo = yo_parts[0] if nseg == 1 else jnp.concatenate(yo_parts, axis=0)
        y = y + yo * e2_ref[:, sl]
        y = y + dskip_ref[:, sl] * conv_ref[:, sl]
        y = y * _silu(z_ref[:, sl])
        y = y * lax.rsqrt(jnp.mean(y * y, axis=-1, keepdims=True) + NORM_EPS) * norm_ref[:, sl]
        y_ref[:, sl] = y.astype(BF16)


def _ssd(proj, l, prm, mats, st_shape, st_prev, n_outer, n_inner, rows, seg_len, token_lo, token_hi,
         conv_state=None, ssm_state=None):
    has_state = ssm_state is not None
    nseg = rows // seg_len
    n = proj.shape[0]
    st_block = (None, nseg, SSD_HEADS, SSD_HEAD_DIM, D_STATE)

    def rowmap(cb):
        return lambda o, c: (o * n_inner + c, cb)

    def st_map(o, c):
        return (l, o, 0, 0, 0)

    in_specs = [pl.BlockSpec((rows, CONV_DIM), rowmap(COL_XBC // CONV_DIM)),
                pl.BlockSpec((rows, D_INNER), rowmap(COL_Z // D_INNER)),
                pl.BlockSpec((rows, LANES), rowmap(COL_DT // LANES)),
                _layer_spec(l, (CONV_WIDTH, CONV_DIM)), _layer_spec(l, (1, CONV_DIM)),
                _layer_spec(l, (1, LANES)), _layer_spec(l, (1, LANES)),
                _layer_spec(l, (1, D_INNER)), _layer_spec(l, (1, D_INNER)),
                _const_spec((rows, rows)), _const_spec((rows, rows)), _const_spec((rows, rows)),
                _const_spec((3 * LANES, D_INNER))]
    args = [proj, proj, proj, prm["conv_w"], prm["conv_b"], prm["dt_bias"], prm["a_log"],
            prm["d_skip"], prm["ssd_norm"], mats[0], mats[1], mats[2], mats[3]]
    if has_state:
        in_specs += [pl.BlockSpec((None, rows, CONV_DIM), lambda o, c: (l, o * n_inner + c, 0)),
                     pl.BlockSpec(st_block, st_map)]
        args += [conv_state, ssm_state]
    aliases = {}
    if st_prev is not None:
        in_specs.append(_ANY)
        args.append(st_prev)
        aliases = {len(args) - 1: 1}
    y, st = pl.pallas_call(
        functools.partial(_ssd_kernel, rows=rows, seg_len=seg_len, has_state=has_state,
                          has_prev=st_prev is not None, token_lo=token_lo, token_hi=token_hi),
        grid=(n_outer, n_inner),
        in_specs=in_specs,
        out_specs=[pl.BlockSpec((rows, D_INNER), rowmap(0)), pl.BlockSpec(st_block, st_map)],
        out_shape=[jax.ShapeDtypeStruct((n, D_INNER), BF16),
                   jax.ShapeDtypeStruct(st_shape, F32)],
        scratch_shapes=[pltpu.VMEM((CONV_DIM // LANES, 16 + 2 * rows, LANES), F32),
                        pltpu.VMEM((rows, CONV_DIM), F32),
                        pltpu.VMEM((rows, D_INNER), F32),
                        pltpu.VMEM((rows, D_INNER), F32),
                        pltpu.VMEM((rows, D_INNER), F32),
                        pltpu.VMEM((rows, D_INNER), BF16),
                        pltpu.VMEM((rows, D_INNER), BF16)],
        input_output_aliases=aliases,
        compiler_params=_cparams(("parallel", "arbitrary")),
        name="ssd_state" if has_state else "ssd_prompt",
    )(*args)
    return y, st


def _attn_prompt_kernel(q0, q1, k0, k1, v0, v1, bias_ref, *rest, dil, seq, keep):
    o_ref, lse_ref, kvt_ref = rest[-3:]
    nblk = seq // (ATTN_BLOCK * dil)
    lo_lanes = lax.broadcasted_iota(jnp.int32, (ATTN_BLOCK, LANES), 1) < ATTN_HEAD_DIM

    tc = min(keep, 512)
    for half, (k, v) in enumerate(((k0, v0), (k1, v1))):
        for c0 in range(0, keep, tc):
            src = pl.ds(seq - keep + c0, tc)
            kvt_ref[half * LANES:(half + 1) * LANES, c0:c0 + tc] = k[src, :].T
            kvt_ref[ATTN_GROUP_WIDTH + half * LANES:ATTN_GROUP_WIDTH + (half + 1) * LANES, c0:c0 + tc] = v[src, :].T

    def rows(start):
        if dil == 1:
            return pl.ds(pl.multiple_of(start, ATTN_BLOCK), ATTN_BLOCK)
        return pl.ds(start, ATTN_BLOCK, stride=dil)

    def body(i, carry):
        r = i // nblk
        blk = i % nblk
        cur = rows(r + blk * ATTN_BLOCK * dil)
        prev = rows(r + jnp.maximum(blk - 1, 0) * ATTN_BLOCK * dil)
        tab = jnp.minimum(blk, 1)
        for half, (q, k, v) in enumerate(((q0, k0, v0), (q1, k1, v1))):
            qb = q[cur, :] * ATTN_SCALE
            kk = jnp.concatenate([k[prev, :], k[cur, :]], axis=0).astype(BF16)
            vv = jnp.concatenate([v[prev, :], v[cur, :]], axis=0).astype(BF16)
            outs, lses = [], []
            for hh in range(2):
                lanes = lo_lanes if hh == 0 else jnp.logical_not(lo_lanes)
                qm = jnp.where(lanes, qb, 0.0).astype(BF16)
                s = _dot_nt(qm, kk) + bias_ref[tab, 2 * half + hh]
                m = jnp.max(s, axis=-1, keepdims=True)
                p = jnp.exp(s - m)
                l = jnp.sum(p, axis=-1, keepdims=True)
                outs.append(_dot(p.astype(BF16), vv) / l)
                lses.append(jnp.broadcast_to(m + jnp.log(l), (ATTN_BLOCK, LANES)))
            o_ref[half, cur, :] = jnp.where(lo_lanes, outs[0], outs[1])
            lse_ref[half, cur, :] = jnp.where(lo_lanes, lses[0], lses[1])
        return carry

    lax.fori_loop(0, dil * nblk, body, 0, unroll=4)


def _attn_prompt(proj, l, bias, kvt_shape, kvt_prev, gi, batch, seq):
    dil = DILATIONS[gi]
    keep = kvt_shape[-1]
    base = (COL_QKV + gi * ATTN_WIDTH) // LANES

    def col(j):
        return pl.BlockSpec((seq, LANES), lambda b: (b, base + j))

    out_spec = pl.BlockSpec((2, seq, LANES), lambda b: (0, b, 0))
    shape = jax.ShapeDtypeStruct((2, batch * seq, LANES), F32)
    prev = [] if kvt_prev is None else [kvt_prev]
    return pl.pallas_call(
        functools.partial(_attn_prompt_kernel, dil=dil, seq=seq, keep=keep),
        grid=(batch,),
        in_specs=[col(0), col(1), col(2), col(3), col(4), col(5),
                  _const_spec((2, HEADS_PER_GROUP, ATTN_BLOCK, 2 * ATTN_BLOCK))] + ([_ANY] if prev else []),
        out_specs=[out_spec, out_spec,
                   pl.BlockSpec((None, None, KV_WIDTH, keep), lambda b: (l, b, 0, 0))],
        out_shape=[shape, shape, jax.ShapeDtypeStruct(kvt_shape, F32)],
        input_output_aliases={7: 2} if prev else {},
        compiler_params=_cparams(("parallel",)),
        name=f"attn_prompt_d{dil}",
    )(proj, proj, proj, proj, proj, proj, bias, *prev)


def _attn_sample_kernel(q0, q1, kn0, kn1, vn0, vn1, cache_ref, bias_c_ref, bias_n_ref, o_ref, lse_ref, *, bt):
    lo8 = lax.broadcasted_iota(jnp.int32, (SEG, LANES), 1) < ATTN_HEAD_DIM
    lo16 = lax.broadcasted_iota(jnp.int32, (2 * SEG, LANES), 1) < ATTN_HEAD_DIM
    top16 = lax.broadcasted_iota(jnp.int32, (2 * SEG, LANES), 0) < SEG
    head_lanes = lo16 == top16

    def body(b, carry):
        seg = pl.ds(pl.multiple_of(b * SEG, SEG), SEG)
        for half, (q, kn, vn) in enumerate(((q0, kn0, vn0), (q1, kn1, vn1))):
            qb = q[seg, :] * ATTN_SCALE
            q16 = jnp.where(head_lanes, jnp.concatenate([qb, qb], axis=0), 0.0).astype(BF16)
            knew = kn[seg, :].astype(BF16)
            vnew = vn[seg, :].astype(BF16)
            kt = cache_ref[b, half * LANES:(half + 1) * LANES, :].astype(BF16)
            vt = cache_ref[b, ATTN_GROUP_WIDTH + half * LANES:
                           ATTN_GROUP_WIDTH + (half + 1) * LANES, :].astype(BF16)
            sc = _dot(q16, kt) + bias_c_ref[half]
            sn = _dot_nt(q16, knew) + bias_n_ref[half]
            m = jnp.maximum(jnp.max(sc, axis=-1, keepdims=True), jnp.max(sn, axis=-1, keepdims=True))
            pc = jnp.exp(sc - m)
            pn = jnp.exp(sn - m)
            l = jnp.sum(pc, axis=-1, keepdims=True) + jnp.sum(pn, axis=-1, keepdims=True)
            o = (_dot_nt(pc.astype(BF16), vt) + _dot(pn.astype(BF16), vnew)) / l
            lse = jnp.broadcast_to(m + jnp.log(l), (2 * SEG, LANES))
            o_ref[half, seg, :] = jnp.where(lo8, o[0:SEG], o[SEG:2 * SEG])
            lse_ref[half, seg, :] = jnp.where(lo8, lse[0:SEG], lse[SEG:2 * SEG])
        return carry

    lax.fori_loop(0, bt, body, 0, unroll=min(bt, 8))


def _attn_sample(proj, l, cache_t, bias_c, bias_n, gi, bt):
    nb = cache_t.shape[1]
    rows_c = cache_t.shape[3]
    base = (COL_QKV + gi * ATTN_WIDTH) // LANES

    def col(j):
        return pl.BlockSpec((bt * SEG, LANES), lambda i: (i, base + j))

    out_spec = pl.BlockSpec((2, bt * SEG, LANES), lambda i: (0, i, 0))
    shape = jax.ShapeDtypeStruct((2, nb * SEG, LANES), F32)
    return pl.pallas_call(
        functools.partial(_attn_sample_kernel, bt=bt),
        grid=(nb // bt,),
        in_specs=[col(0), col(1), col(2), col(3), col(4), col(5),
                  pl.BlockSpec((None, bt, KV_WIDTH, rows_c), lambda i: (l, i, 0, 0)),
                  _const_spec((2, 2 * SEG, rows_c)),
                  _const_spec((2, 2 * SEG, SEG))],
        out_specs=[out_spec, out_spec],
        out_shape=[shape, shape],
        compiler_params=_cparams(("parallel",)),
        name=f"attn_sample_d{DILATIONS[gi]}",
    )(proj, proj, proj, proj, proj, proj, cache_t, bias_c, bias_n)


def _mix_out_kernel(x_ref, y_ref, gs_ref, ga_ref, o0, o1, o2, l0, l1, l2, wso_ref, wao_ref, wo_ref, out_ref,
                    *, token_lo, token_hi, seg_len):
    ssd_out = _dot(y_ref[...], wso_ref[...])
    halves = []
    for half in range(2):
        ls = [l0[half], l1[half], l2[half]]
        os_ = [o0[half], o1[half], o2[half]]
        m = jnp.maximum(jnp.maximum(ls[0], ls[1]), ls[2])
        es = [jnp.exp(l - m) for l in ls]
        den = es[0] + es[1] + es[2]
        o = (es[0] / den) * os_[0] + (es[1] / den) * os_[1] + (es[2] / den) * os_[2]
        halves.append(o.astype(BF16))
    attn_out = _dot(jnp.concatenate(halves, axis=1), wao_ref[...])
    merged = jax.nn.sigmoid(gs_ref[...]) * ssd_out + jax.nn.sigmoid(ga_ref[...]) * attn_out
    upd = _dot(merged.astype(BF16), wo_ref[...])
    if seg_len is not None:
        rows = x_ref.shape[0]
        ris = lax.broadcasted_iota(jnp.int32, (rows, 1), 0) % seg_len
        upd = jnp.where((ris >= token_lo) & (ris < token_hi), upd, 0.0)
    out_ref[...] = x_ref[...] + upd


def _mix_out(x, y, proj, attn, l, prm, tm, seg_len=None):
    n = x.shape[0]
    half_spec = pl.BlockSpec((2, tm, LANES), lambda i: (0, i, 0))
    (o0, l0), (o1, l1), (o2, l2) = attn
    return pl.pallas_call(
        functools.partial(_mix_out_kernel, token_lo=SEG_T0, token_hi=SEG_T0 + 4, seg_len=seg_len),
        grid=(n // tm,),
        in_specs=[pl.BlockSpec((tm, D_MODEL), lambda i: (i, 0)),
                  pl.BlockSpec((tm, D_INNER), lambda i: (i, 0)),
                  pl.BlockSpec((tm, D_MODEL), lambda i: (i, COL_GS // D_MODEL)),
                  pl.BlockSpec((tm, D_MODEL), lambda i: (i, COL_GA // D_MODEL)),
                  half_spec, half_spec, half_spec, half_spec, half_spec, half_spec,
                  _layer_spec(l, (D_INNER, D_MODEL), single=True),
                  _layer_spec(l, (ATTN_GROUP_WIDTH, D_MODEL), single=True),
                  _layer_spec(l, (D_MODEL, D_MODEL), single=True)],
        out_specs=pl.BlockSpec((tm, D_MODEL), lambda i: (i, 0)),
        out_shape=jax.ShapeDtypeStruct((n, D_MODEL), F32),
        compiler_params=_cparams(("parallel",)),
        name="mix_out",
    )(x, y, proj, proj, o0, o1, o2, l0, l1, l2, prm["w_ssd_out"], prm["w_attn_out"], prm["w_o"])


def _rel_bucket(dist):
    max_exact = REL_BUCKETS // 2
    d = jnp.maximum(dist, 1).astype(F32)
    large = max_exact + (jnp.log(d / max_exact) / math.log(REL_MAX_DIST / max_exact)
                         * (REL_BUCKETS - max_exact)).astype(jnp.int32)
    large = jnp.minimum(large, REL_BUCKETS - 1)
    return jnp.where(dist < max_exact, dist, large)


def _bias_lookup(tab, dist):
    onehot = (_rel_bucket(dist)[..., None] == jnp.arange(REL_BUCKETS)).astype(F32)
    return jnp.einsum("...k,kh->h...", onehot, tab.astype(F32), precision=lax.Precision.HIGHEST)


def _prompt_bias(rel_bias, gi):
    dil, reach = DILATIONS[gi], WINDOWS[gi] // DILATIONS[gi]
    blk = ATTN_BLOCK
    step = (jnp.arange(blk)[:, None] + blk) - jnp.arange(2 * blk)[None, :]
    tab = rel_bias[:, gi * HEADS_PER_GROUP:(gi + 1) * HEADS_PER_GROUP]
    bias = _bias_lookup(tab, jnp.maximum(step, 0) * dil)
    valid = (step >= 0) & (step <= reach)
    first = valid & (jnp.arange(2 * blk)[None, :] >= blk)
    neg = jnp.float32(-jnp.inf)
    return jnp.stack([jnp.where(first[None], bias, neg), jnp.where(valid[None], bias, neg)])


def _sample_bias(rel_bias, gi, row_pos):
    dil, window = DILATIONS[gi], WINDOWS[gi]
    tab = rel_bias[:, gi * HEADS_PER_GROUP:(gi + 1) * HEADS_PER_GROUP]
    seg_row = jnp.arange(SEG)
    is_tok = (seg_row >= SEG_T0) & (seg_row < SEG_T0 + 4)
    t = jnp.where(is_tok, seg_row - SEG_T0, 0)
    neg = jnp.float32(-jnp.inf)

    def table(dist, ok):
        ok = ok & (dist >= 0) & (dist % dil == 0) & (dist // dil <= window // dil)
        tbl = jnp.where(ok[None], _bias_lookup(tab, jnp.maximum(dist, 0)), neg)
        return tbl.reshape(2, 2 * SEG, dist.shape[1])

    dist_c = t[:, None] - row_pos[None, :]
    bias_c = table(dist_c, jnp.ones(dist_c.shape, bool))
    key_tok = seg_row - SEG_T0
    dist_n = t[:, None] - key_tok[None, :]
    bias_n = table(dist_n, jnp.broadcast_to(is_tok[None, :], dist_n.shape))
    return bias_c, bias_n


def _ssd_mats(rows, seg_len):
    l = jnp.arange(rows)[:, None]
    s = jnp.arange(rows)[None, :]
    same = (l // seg_len) == (s // seg_len)
    causal = same & (s <= l)
    head = jnp.arange(LANES)[:, None]
    chan = jnp.arange(D_INNER)[None, :] // SSD_HEAD_DIM
    spread = jnp.tile((head == chan).astype(BF16), (3, 1))
    return causal.astype(F32), same.astype(F32), causal.astype(F32), spread


def _kv_rows_minor(c):
    d, b, rows = c.shape[:3]
    return c.transpose(0, 1, 3, 4, 5, 2).reshape(d, b, KV_WIDTH, rows)


def kernel(x_prompt, x_sample, cache_kv_w128, cache_kv_w512, cache_kv_w2048, state_conv, state_ssm, rel_bias,
           ffn1_norm, ffn1_w13, ffn1_w2, mix_norm, w_in, conv_w, conv_b, dt_bias, a_log, d_skip, ssd_norm,
           w_ssd_out, w_attn_out, w_o, ffn2_norm, ffn2_w13, ffn2_w2, final_norm):
    batch, seq, _ = x_prompt.shape
    dec_batch, dec_seq, _ = x_sample.shape
    n_p = batch * seq
    n_s = dec_batch * SEG
    tm_p, tm_s = 512, min(512, n_s)
    tmi_p, tmi_s = 2048, min(1024, n_s)
    tok_lo, tok_hi = SEG_T0, SEG_T0 + dec_seq

    w_t = jnp.swapaxes(w_in, 1, 2)
    z_w, xbc_w, dt_w, q_w, k_w, v_w, gs_w, ga_w = jnp.split(
        w_t, [2048, 6144, 6176, 6944, 7712, 8480, 9504], axis=1)
    qkv_w = []
    for gi in range(N_DIL_GROUPS):
        sl = slice(gi * ATTN_GROUP_WIDTH, (gi + 1) * ATTN_GROUP_WIDTH)
        qkv_w += [q_w[:, sl], k_w[:, sl], v_w[:, sl]]
    pad_w = jnp.zeros((DEPTH, PROJ_W - COL_DT - SSD_HEADS, D_MODEL), w_in.dtype)
    w_in_t = jnp.concatenate([xbc_w, z_w, gs_w, ga_w] + qkv_w + [dt_w, pad_w], axis=1).astype(BF16)

    def pad_heads(v):
        return jnp.pad(v, ((0, 0), (0, LANES - SSD_HEADS)))[:, None, :]

    prm = dict(
        ffn1_norm=ffn1_norm[:, None], ffn1_w13=ffn1_w13.astype(BF16), ffn1_w2=ffn1_w2.astype(BF16),
        mix_norm=mix_norm[:, None], conv_w=conv_w, conv_b=conv_b[:, None],
        dt_bias=pad_heads(dt_bias), a_log=pad_heads(a_log),
        d_skip=jnp.repeat(d_skip, SSD_HEAD_DIM, axis=1)[:, None], ssd_norm=ssd_norm[:, None],
        w_ssd_out=w_ssd_out.astype(BF16), w_attn_out=w_attn_out.astype(BF16), w_o=w_o.astype(BF16),
        ffn2_norm=ffn2_norm[:, None], ffn2_w13=ffn2_w13.astype(BF16), ffn2_w2=ffn2_w2.astype(BF16))

    mats_p = _ssd_mats(SSD_CHUNK, SSD_CHUNK)
    rows_s = 8 * SEG
    mats_s = _ssd_mats(rows_s, SEG)
    bias_p = [_prompt_bias(rel_bias, gi) for gi in range(N_DIL_GROUPS)]
    caches = (cache_kv_w128, cache_kv_w512, cache_kv_w2048)
    bias_s, caches_t = [], []
    for gi in range(N_DIL_GROUPS):
        lb = caches[gi].shape[2]
        bias_s.append(_sample_bias(rel_bias, gi, jnp.arange(lb) - lb))
        caches_t.append(_kv_rows_minor(caches[gi]))
    conv_state_rows = jnp.pad(state_conv, ((0, 0), (0, 0), (0, SEG - (CONV_WIDTH - 1)), (0, 0))).reshape(
        DEPTH, n_s, CONV_DIM)
    sample_bt = tuple(min(bt, dec_batch) for bt in (32, 8, 2))

    xp = x_prompt.reshape(n_p, D_MODEL)
    xs = jnp.pad(x_sample, ((0, 0), (SEG_T0, SEG - SEG_T0 - dec_seq), (0, 0))).reshape(n_s, D_MODEL)
    ssm_p_shape = (DEPTH, batch, SSD_HEADS, SSD_HEAD_DIM, D_STATE)
    ssm_s_shape = (DEPTH, dec_batch, SSD_HEADS, SSD_HEAD_DIM, D_STATE)
    kvt_shapes = [(DEPTH, batch, KV_WIDTH, min(WINDOWS[gi], seq)) for gi in range(N_DIL_GROUPS)]
    ssm_p = ssm_s = None
    kvt_p = [None] * N_DIL_GROUPS
    fn = final_norm[None]

    kv_s = [[], [], []]
    conv_p, conv_s = [], []
    for l in range(DEPTH):
        xp = _ffn(xp, l, prm["ffn1_norm"], prm["ffn1_w13"], prm["ffn1_w2"], tm_p)
        xs = _ffn(xs, l, prm["ffn1_norm"], prm["ffn1_w13"], prm["ffn1_w2"], tm_s)
        proj_p = _inproj(xp, l, prm["mix_norm"], w_in_t, tmi_p)
        proj_s = _inproj(xs, l, prm["mix_norm"], w_in_t, tmi_s)

        y_p, ssm_p = _ssd(proj_p, l, prm, mats_p, ssm_p_shape, ssm_p, batch, seq // SSD_CHUNK, SSD_CHUNK,
                          SSD_CHUNK, 0, SSD_CHUNK)
        y_s, ssm_s = _ssd(proj_s, l, prm, mats_s, ssm_s_shape, ssm_s, n_s // rows_s, 1, rows_s, SEG,
                          tok_lo, tok_hi, conv_state=conv_state_rows, ssm_state=state_ssm)
        attn_p = []
        for gi in range(N_DIL_GROUPS):
            o, lse, kvt_p[gi] = _attn_prompt(proj_p, l, bias_p[gi], kvt_shapes[gi], kvt_p[gi], gi, batch, seq)
            attn_p.append((o, lse))
        attn_s = [_attn_sample(proj_s, l, caches_t[gi], bias_s[gi][0], bias_s[gi][1], gi, sample_bt[gi])
                  for gi in range(N_DIL_GROUPS)]

        xp = _mix_out(xp, y_p, proj_p, attn_p, l, prm, tm_p)
        xs = _mix_out(xs, y_s, proj_s, attn_s, l, prm, tm_s, seg_len=SEG)
        last = fn if l == DEPTH - 1 else None
        xp = _ffn(xp, l, prm["ffn2_norm"], prm["ffn2_w13"], prm["ffn2_w2"], tm_p, final_gain=last)
        xs = _ffn(xs, l, prm["ffn2_norm"], prm["ffn2_w13"], prm["ffn2_w2"], tm_s, final_gain=last)

        pp = proj_p.reshape(batch, seq, PROJ_W)
        ps = proj_s.reshape(dec_batch, SEG, PROJ_W)
        conv_p.append(pp[:, seq - (CONV_WIDTH - 1):, COL_XBC:COL_XBC + CONV_DIM])
        conv_s.append(ps[:, tok_hi - (CONV_WIDTH - 1):tok_hi, COL_XBC:COL_XBC + CONV_DIM])
        for gi in range(N_DIL_GROUPS):
            c0 = COL_QKV + gi * ATTN_WIDTH + ATTN_GROUP_WIDTH
            kv_s[gi].append(ps[:, tok_lo:tok_hi, c0:c0 + KV_WIDTH].reshape(
                dec_batch, dec_seq, 2, HEADS_PER_GROUP, ATTN_HEAD_DIM))

    def kv_prompt(kvt):
        keep = kvt.shape[-1]
        return kvt.reshape(DEPTH, batch, 2, HEADS_PER_GROUP, ATTN_HEAD_DIM, keep).transpose(0, 1, 5, 2, 3, 4)

    y_prompt = xp.reshape(batch, seq, D_MODEL)
    y_sample = xs.reshape(dec_batch, SEG, D_MODEL)[:, tok_lo:tok_hi]
    return (y_prompt, y_sample,
            kv_prompt(kvt_p[0]), kv_prompt(kvt_p[1]), kv_prompt(kvt_p[2]), jnp.stack(conv_p), ssm_p,
            jnp.stack(kv_s[0]), jnp.stack(kv_s[1]), jnp.stack(kv_s[2]), jnp.stack(conv_s), ssm_s)
```

```python
import functools
import math

import jax
import jax.numpy as jnp
from jax import lax
from jax.experimental import pallas as pl
from jax.experimental.pallas import tpu as pltpu

F32 = jnp.float32
BF16 = jnp.bfloat16

D_MODEL = 1024
DEPTH = 4
D_INNER = 2048
SSD_HEAD_DIM = 64
SSD_HEADS = 32
SSD_GROUPS = 8
SSD_HPG = 4
D_STATE = 128
CONV_WIDTH = 4
CONV_DIM = 4096
SSD_CHUNK = 128
ATTN_HEAD_DIM = 64
HEADS_PER_GROUP = 4
WINDOWS = (128, 512, 2048)
DILATIONS = (1, 4, 16)
N_DIL_GROUPS = 3
ATTN_WIDTH = 768
ATTN_GROUP_WIDTH = HEADS_PER_GROUP * ATTN_HEAD_DIM
KV_WIDTH = 2 * ATTN_GROUP_WIDTH
ATTN_BLOCK = 128
ATTN_SCALE = ATTN_HEAD_DIM ** -0.5
REL_BUCKETS = 32
REL_MAX_DIST = 2048
D_FF = 2816
FFN_RES = 0.5
NORM_EPS = 1e-6
LOG2E = 1.4426950408889634

LANES = 128
SEG = 8
SEG_T0 = CONV_WIDTH - 1

COL_XBC = 0
COL_Z = COL_XBC + CONV_DIM
COL_GS = COL_Z + D_INNER
COL_GA = COL_GS + D_MODEL
COL_QKV = COL_GA + D_MODEL
COL_DT = COL_QKV + 3 * ATTN_WIDTH
PROJ_W = 10752
PROJ_TN = 768

VMEM_LIMIT = 56 * 1024 * 1024


def _cparams(sem):
    return pltpu.CompilerParams(dimension_semantics=sem, vmem_limit_bytes=VMEM_LIMIT)


def _layer_spec(l, shape, single=False):
    nd = len(shape)
    kw = dict(pipeline_mode=pl.Buffered(1)) if single else {}
    return pl.BlockSpec((None,) + tuple(shape), lambda *_: (l,) + (0,) * nd, **kw)


def _const_spec(shape):
    nd = len(shape)
    return pl.BlockSpec(tuple(shape), lambda *_: (0,) * nd)


_ANY = pl.BlockSpec(memory_space=pl.ANY)


def _rms(x, g):
    return x * lax.rsqrt(jnp.mean(x * x, axis=-1, keepdims=True) + NORM_EPS) * g


def _silu(x):
    return x * jax.nn.sigmoid(x)


def _dot(a, b):
    return jnp.dot(a, b, preferred_element_type=F32)


def _dot_nt(a, b):
    return lax.dot_general(a, b, (((1,), (1,)), ((), ())), preferred_element_type=F32)


def _dot_tn(a, b):
    return lax.dot_general(a, b, (((0,), (0,)), ((), ())), preferred_element_type=F32)


def _ffn_kernel(x_ref, g_ref, w13_ref, w2_ref, *rest, n_chunks):
    o_ref = rest[-1]
    x = x_ref[...]
    xn = _rms(x, g_ref[...]).astype(BF16)
    tf = D_FF // n_chunks
    acc = None
    for c in range(n_chunks):
        a = _dot(xn, w13_ref[:, c * tf:(c + 1) * tf])
        b = _dot(xn, w13_ref[:, D_FF + c * tf:D_FF + (c + 1) * tf])
        h = (_silu(a) * b).astype(BF16)
        d = _dot(h, w2_ref[c * tf:(c + 1) * tf, :])
        acc = d if acc is None else acc + d
    y = x + FFN_RES * acc
    if len(rest) == 2:
        y = _rms(y, rest[0][...])
    o_ref[...] = y


def _ffn(x, l, g, w13, w2, tm, final_gain=None):
    n = x.shape[0]
    extra = [] if final_gain is None else [final_gain]
    return pl.pallas_call(
        functools.partial(_ffn_kernel, n_chunks=2),
        grid=(n // tm,),
        in_specs=[pl.BlockSpec((tm, D_MODEL), lambda i: (i, 0)),
                  _layer_spec(l, (1, D_MODEL)),
                  _layer_spec(l, (D_MODEL, 2 * D_FF), single=True),
                  _layer_spec(l, (D_FF, D_MODEL), single=True)] + [_const_spec((1, D_MODEL))] * len(extra),
        out_specs=pl.BlockSpec((tm, D_MODEL), lambda i: (i, 0)),
        out_shape=jax.ShapeDtypeStruct((n, D_MODEL), F32),
        compiler_params=_cparams(("parallel",)),
        name="ffn",
    )(x, g, w13, w2, *extra)


def _inproj_kernel(x_ref, g_ref, w_ref, o_ref, xn_ref):
    @pl.when(pl.program_id(1) == 0)
    def _():
        xn_ref[...] = _rms(x_ref[...], g_ref[...]).astype(BF16)

    o_ref[...] = _dot_nt(xn_ref[...], w_ref[...])


def _inproj(x, l, g, w_t, tm):
    n = x.shape[0]
    return pl.pallas_call(
        _inproj_kernel,
        grid=(n // tm, PROJ_W // PROJ_TN),
        in_specs=[pl.BlockSpec((tm, D_MODEL), lambda i, j: (i, 0)),
                  _layer_spec(l, (1, D_MODEL)),
                  pl.BlockSpec((None, PROJ_TN, D_MODEL), lambda i, j: (l, j, 0))],
        out_specs=pl.BlockSpec((tm, PROJ_TN), lambda i, j: (i, j)),
        out_shape=jax.ShapeDtypeStruct((n, PROJ_W), F32),
        scratch_shapes=[pltpu.VMEM((tm, D_MODEL), BF16)],
        compiler_params=_cparams(("parallel", "arbitrary")),
        name="inproj",
    )(x, g, w_t)


def _ssd_kernel(*refs, rows, seg_len, has_state, has_prev, token_lo, token_hi):
    (xbc_ref, z_ref, dt_ref, cw_ref, cb_ref, dtb_ref, alog_ref, dskip_ref, norm_ref,
     csm_ref, totm_ref, mask_ref, spread_ref) = refs[:13]
    rest = refs[13:]
    if has_state:
        cst_ref, st_in_ref = rest[:2]
        rest = rest[2:]
    if has_prev:
        rest = rest[1:]
    y_ref, st_ref, xpad_ref, conv_ref, e2_ref, xdt_ref, w_ref = rest
    nseg = rows // seg_len
    c = pl.program_id(1)

    @pl.when(c == 0)
    def _():
        xpad_ref[:, 0:16, :] = jnp.zeros((CONV_DIM // LANES, 16, LANES), F32)
        if has_state:
            st_ref[...] = st_in_ref[...]
        else:
            st_ref[...] = jnp.zeros(st_ref.shape, F32)

    row_in_seg = lax.broadcasted_iota(jnp.int32, (rows, 1), 0) % seg_len
    xbc = xbc_ref[...]
    if has_state:
        xbc = jnp.where(row_in_seg < SEG_T0, cst_ref[...], xbc)
    for s in range(CONV_DIM // LANES):
        sl = slice(s * LANES, (s + 1) * LANES)
        xs_ = xbc[:, sl]
        xpad_ref[s, pl.ds(16, rows, stride=2), :] = xs_
        conv = cb_ref[:, sl] + xs_ * cw_ref[CONV_WIDTH - 1:CONV_WIDTH, sl]
        for i in range(CONV_WIDTH - 1):
            back = CONV_WIDTH - 1 - i
            conv = conv + xpad_ref[s, pl.ds(16 - 2 * back, rows, stride=2), :] * cw_ref[i:i + 1, sl]
        xpad_ref[s, pl.ds(0, 8, stride=2), :] = xs_[rows - 8:rows]
        conv_ref[:, sl] = _silu(conv)

    is_token = (row_in_seg >= token_lo) & (row_in_seg < token_hi)
    dtr = dt_ref[...] + dtb_ref[...]
    dt = jnp.maximum(dtr, 0.0) + jnp.log1p(jnp.exp(-jnp.abs(dtr)))
    dt = jnp.where(is_token, dt, 0.0)
    a = dt * (-jnp.exp(alog_ref[...]))
    a_cs = jnp.dot(csm_ref[...], a, preferred_element_type=F32, precision=lax.Precision.HIGHEST)
    a_tot = jnp.dot(totm_ref[...], a, preferred_element_type=F32, precision=lax.Precision.HIGHEST)
    a_l2 = a_cs * LOG2E
    if rows < LANES:
        a_l2_sq = jnp.concatenate([a_l2, jnp.zeros((LANES - rows, LANES), F32)], axis=0)
    else:
        a_l2_sq = a_l2
    a_l2_t = a_l2_sq.T
    cdec = jnp.exp(a_tot)
    mask = mask_ref[...] > 0.5

    def split3(v):
        hi = v.astype(BF16)
        r1 = v - hi.astype(F32)
        mid = r1.astype(BF16)
        lo = (r1 - mid.astype(F32)).astype(BF16)
        return jnp.concatenate([hi, mid, lo], axis=1)

    lhs = jnp.concatenate([split3(dt), split3(dt * jnp.exp(a_tot - a_cs)), split3(jnp.exp(a_cs))], axis=0)
    for g in range(SSD_GROUPS):
        sl = slice(g * SSD_HPG * SSD_HEAD_DIM, (g + 1) * SSD_HPG * SSD_HEAD_DIM)
        spread = _dot(lhs, spread_ref[:, sl])
        xs = conv_ref[:, sl]
        xdt_ref[:, sl] = (xs * spread[0:rows]).astype(BF16)
        w_ref[:, sl] = (xs * spread[rows:2 * rows]).astype(BF16)
        e2_ref[:, sl] = spread[2 * rows:3 * rows]

    gw = SSD_HPG * SSD_HEAD_DIM
    head_of_lane = lax.broadcasted_iota(jnp.int32, (rows, gw), 1) // SSD_HEAD_DIM
    for g in range(SSD_GROUPS):
        sl = slice(g * gw, (g + 1) * gw)
        bg = conv_ref[:, D_INNER + g * D_STATE:D_INNER + (g + 1) * D_STATE].astype(BF16)
        cg = conv_ref[:, D_INNER + SSD_GROUPS * D_STATE + g * D_STATE:
                      D_INNER + SSD_GROUPS * D_STATE + (g + 1) * D_STATE].astype(BF16)
        cbm = _dot_nt(cg, bg)
        xd_g = xdt_ref[:, sl]
        m_parts, bd_parts = [], []
        for r in range(SSD_HPG):
            h = g * SSD_HPG + r
            seg = a_l2[:, h:h + 1] - a_l2_t[h:h + 1, 0:rows]
            decay = jnp.exp2(jnp.where(mask, seg, -jnp.inf))
            m_parts.append((cbm * decay).astype(BF16))
            bd_parts.append(jnp.where(head_of_lane == r, xd_g, jnp.zeros_like(xd_g)))
        y = _dot(jnp.concatenate(m_parts, axis=1), jnp.concatenate(bd_parts, axis=0))
        yo_parts = []
        for j in range(nseg):
            r0 = j * seg_len
            s_old = st_ref[j, g * SSD_HPG:(g + 1) * SSD_HPG].reshape(gw, D_STATE)
            yo_parts.append(_dot_nt(cg[r0:r0 + seg_len], s_old.astype(BF16)))
            dec = jnp.concatenate(
                [jnp.broadcast_to(cdec[r0:r0 + 1, g * SSD_HPG + r:g * SSD_HPG + r + 1], (SSD_HEAD_DIM, D_STATE))
                 for r in range(SSD_HPG)], axis=0)
            s_new = s_old * dec + _dot_tn(w_ref[r0:r0 + seg_len, sl], bg[r0:r0 + seg_len])
            st_ref[j, g * SSD_HPG:(g + 1) * SSD_HPG] = s_new.reshape(SSD_HPG, SSD_HEAD_DIM, D_STATE)
        yo = yo_parts[0] if nseg == 1 else jnp.concatenate(yo_parts, axis=0)
        y = y + yo * e2_ref[:, sl]
        y = y + dskip_ref[:, sl] * conv_ref[:, sl]
        y = y * _silu(z_ref[:, sl])
        y = y * lax.rsqrt(jnp.mean(y * y, axis=-1, keepdims=True) + NORM_EPS) * norm_ref[:, sl]
        y_ref[:, sl] = y.astype(BF16)


def _ssd(proj, l, prm, mats, st_shape, st_prev, n_outer, n_inner, rows, seg_len, token_lo, token_hi,
         conv_state=None, ssm_state=None):
    has_state = ssm_state is not None
    nseg = rows // seg_len
    n = proj.shape[0]
    st_block = (None, nseg, SSD_HEADS, SSD_HEAD_DIM, D_STATE)

    def rowmap(cb):
        return lambda o, c: (o * n_inner + c, cb)

    def st_map(o, c):
        return (l, o, 0, 0, 0)

    in_specs = [pl.BlockSpec((rows, CONV_DIM), rowmap(COL_XBC // CONV_DIM)),
                pl.BlockSpec((rows, D_INNER), rowmap(COL_Z // D_INNER)),
                pl.BlockSpec((rows, LANES), rowmap(COL_DT // LANES)),
                _layer_spec(l, (CONV_WIDTH, CONV_DIM)), _layer_spec(l, (1, CONV_DIM)),
                _layer_spec(l, (1, LANES)), _layer_spec(l, (1, LANES)),
                _layer_spec(l, (1, D_INNER)), _layer_spec(l, (1, D_INNER)),
                _const_spec((rows, rows)), _const_spec((rows, rows)), _const_spec((rows, rows)),
                _const_spec((3 * LANES, D_INNER))]
    args = [proj, proj, proj, prm["conv_w"], prm["conv_b"], prm["dt_bias"], prm["a_log"],
            prm["d_skip"], prm["ssd_norm"], mats[0], mats[1], mats[2], mats[3]]
    if has_state:
        in_specs += [pl.BlockSpec((None, rows, CONV_DIM), lambda o, c: (l, o * n_inner + c, 0)),
                     pl.BlockSpec(st_block, st_map)]
        args += [conv_state, ssm_state]
    aliases = {}
    if st_prev is not None:
        in_specs.append(_ANY)
        args.append(st_prev)
        aliases = {len(args) - 1: 1}
    y, st = pl.pallas_call(
        functools.partial(_ssd_kernel, rows=rows, seg_len=seg_len, has_state=has_state,
                          has_prev=st_prev is not None, token_lo=token_lo, token_hi=token_hi),
        grid=(n_outer, n_inner),
        in_specs=in_specs,
        out_specs=[pl.BlockSpec((rows, D_INNER), rowmap(0)), pl.BlockSpec(st_block, st_map)],
        out_shape=[jax.ShapeDtypeStruct((n, D_INNER), BF16),
                   jax.ShapeDtypeStruct(st_shape, F32)],
        scratch_shapes=[pltpu.VMEM((CONV_DIM // LANES, 16 + 2 * rows, LANES), F32),
                        pltpu.VMEM((rows, CONV_DIM), F32),
                        pltpu.VMEM((rows, D_INNER), F32),
                        pltpu.VMEM((rows, D_INNER), BF16),
                        pltpu.VMEM((rows, D_INNER), BF16)],
        input_output_aliases=aliases,
        compiler_params=_cparams(("parallel", "arbitrary")),
        name="ssd_state" if has_state else "ssd_prompt",
    )(*args)
    return y, st


def _attn_prompt_kernel(q0, q1, k0, k1, v0, v1, bias_ref, *rest, dil, seq, keep):
    o_ref, lse_ref, kvt_ref = rest[-3:]
    nblk = seq // (ATTN_BLOCK * dil)
    lo_lanes = lax.broadcasted_iota(jnp.int32, (ATTN_BLOCK, LANES), 1) < ATTN_HEAD_DIM

    tc = min(keep, 512)
    for half, (k, v) in enumerate(((k0, v0), (k1, v1))):
        for c0 in range(0, keep, tc):
            src = pl.ds(seq - keep + c0, tc)
            kvt_ref[half * LANES:(half + 1) * LANES, c0:c0 + tc] = k[src, :].T
            kvt_ref[ATTN_GROUP_WIDTH + half * LANES:ATTN_GROUP_WIDTH + (half + 1) * LANES, c0:c0 + tc] = v[src, :].T

    def rows(start):
        if dil == 1:
            return pl.ds(pl.multiple_of(start, ATTN_BLOCK), ATTN_BLOCK)
        return pl.ds(start, ATTN_BLOCK, stride=dil)

    def body(i, carry):
        r = i // nblk
        blk = i % nblk
        cur = rows(r + blk * ATTN_BLOCK * dil)
        prev = rows(r + jnp.maximum(blk - 1, 0) * ATTN_BLOCK * dil)
        tab = jnp.minimum(blk, 1)
        for half, (q, k, v) in enumerate(((q0, k0, v0), (q1, k1, v1))):
            qb = q[cur, :] * ATTN_SCALE
            kk = jnp.concatenate([k[prev, :], k[cur, :]], axis=0).astype(BF16)
            vv = jnp.concatenate([v[prev, :], v[cur, :]], axis=0).astype(BF16)
            outs, lses = [], []
            for hh in range(2):
                lanes = lo_lanes if hh == 0 else jnp.logical_not(lo_lanes)
                qm = jnp.where(lanes, qb, 0.0).astype(BF16)
                s = _dot_nt(qm, kk) + bias_ref[tab, 2 * half + hh]
                m = jnp.max(s, axis=-1, keepdims=True)
                p = jnp.exp(s - m)
                l = jnp.sum(p, axis=-1, keepdims=True)
                outs.append(_dot(p.astype(BF16), vv) / l)
                lses.append(jnp.broadcast_to(m + jnp.log(l), (ATTN_BLOCK, LANES)))
            o_ref[half, cur, :] = jnp.where(lo_lanes, outs[0], outs[1])
            lse_ref[half, cur, :] = jnp.where(lo_lanes, lses[0], lses[1])
        return carry

    lax.fori_loop(0, dil * nblk, body, 0, unroll=4)


def _attn_prompt(proj, l, bias, kvt_shape, kvt_prev, gi, batch, seq):
    dil = DILATIONS[gi]
    keep = kvt_shape[-1]
    base = (COL_QKV + gi * ATTN_WIDTH) // LANES

    def col(j):
        return pl.BlockSpec((seq, LANES), lambda b: (b, base + j))

    out_spec = pl.BlockSpec((2, seq, LANES), lambda b: (0, b, 0))
    shape = jax.ShapeDtypeStruct((2, batch * seq, LANES), F32)
    prev = [] if kvt_prev is None else [kvt_prev]
    return pl.pallas_call(
        functools.partial(_attn_prompt_kernel, dil=dil, seq=seq, keep=keep),
        grid=(batch,),
        in_specs=[col(0), col(1), col(2), col(3), col(4), col(5),
                  _const_spec((2, HEADS_PER_GROUP, ATTN_BLOCK, 2 * ATTN_BLOCK))] + ([_ANY] if prev else []),
        out_specs=[out_spec, out_spec,
                   pl.BlockSpec((None, None, KV_WIDTH, keep), lambda b: (l, b, 0, 0))],
        out_shape=[shape, shape, jax.ShapeDtypeStruct(kvt_shape, F32)],
        input_output_aliases={7: 2} if prev else {},
        compiler_params=_cparams(("parallel",)),
        name=f"attn_prompt_d{dil}",
    )(proj, proj, proj, proj, proj, proj, bias, *prev)


def _attn_sample_kernel(*refs, bt):
    n_in = 9 * N_DIL_GROUPS
    ins, outs = refs[:n_in], refs[n_in:]
    lo8 = lax.broadcasted_iota(jnp.int32, (SEG, LANES), 1) < ATTN_HEAD_DIM
    lo16 = lax.broadcasted_iota(jnp.int32, (2 * SEG, LANES), 1) < ATTN_HEAD_DIM
    top16 = lax.broadcasted_iota(jnp.int32, (2 * SEG, LANES), 0) < SEG
    head_lanes = lo16 == top16

    for b in range(bt):
        seg = pl.ds(b * SEG, SEG)
        for gi in range(N_DIL_GROUPS):
            q0, q1, kn0, kn1, vn0, vn1, cache_ref, bias_c_ref, bias_n_ref = ins[9 * gi:9 * gi + 9]
            o_ref, lse_ref = outs[2 * gi:2 * gi + 2]
            for half, (q, kn, vn) in enumerate(((q0, kn0, vn0), (q1, kn1, vn1))):
                qb = q[seg, :] * ATTN_SCALE
                q16 = jnp.where(head_lanes, jnp.concatenate([qb, qb], axis=0), 0.0).astype(BF16)
                knew = kn[seg, :].astype(BF16)
                vnew = vn[seg, :].astype(BF16)
                kt = cache_ref[b, half * LANES:(half + 1) * LANES, :].astype(BF16)
                vt = cache_ref[b, ATTN_GROUP_WIDTH + half * LANES:
                               ATTN_GROUP_WIDTH + (half + 1) * LANES, :].astype(BF16)
                sc = _dot(q16, kt) + bias_c_ref[half]
                sn = _dot_nt(q16, knew) + bias_n_ref[half]
                m = jnp.maximum(jnp.max(sc, axis=-1, keepdims=True), jnp.max(sn, axis=-1, keepdims=True))
                pc = jnp.exp(sc - m)
                pn = jnp.exp(sn - m)
                l = jnp.sum(pc, axis=-1, keepdims=True) + jnp.sum(pn, axis=-1, keepdims=True)
                o = (_dot_nt(pc.astype(BF16), vt) + _dot(pn.astype(BF16), vnew)) / l
                lse = jnp.broadcast_to(m + jnp.log(l), (2 * SEG, LANES))
                o_ref[half, seg, :] = jnp.where(lo8, o[0:SEG], o[SEG:2 * SEG])
                lse_ref[half, seg, :] = jnp.where(lo8, lse[0:SEG], lse[SEG:2 * SEG])


def _attn_sample(proj, l, caches_t, biases, bt):
    nb = caches_t[0].shape[1]
    in_specs, args = [], []
    for gi in range(N_DIL_GROUPS):
        rows_c = caches_t[gi].shape[3]
        base = (COL_QKV + gi * ATTN_WIDTH) // LANES
        in_specs += [pl.BlockSpec((bt * SEG, LANES), lambda i, c=base + j: (i, c)) for j in range(6)]
        in_specs += [pl.BlockSpec((None, bt, KV_WIDTH, rows_c), lambda i: (l, i, 0, 0)),
                     _const_spec((2, 2 * SEG, rows_c)), _const_spec((2, 2 * SEG, SEG))]
        args += [proj] * 6 + [caches_t[gi], biases[gi][0], biases[gi][1]]
    out_spec = pl.BlockSpec((2, bt * SEG, LANES), lambda i: (0, i, 0))
    shape = jax.ShapeDtypeStruct((2, nb * SEG, LANES), F32)
    outs = pl.pallas_call(
        functools.partial(_attn_sample_kernel, bt=bt),
        grid=(nb // bt,),
        in_specs=in_specs,
        out_specs=[out_spec] * (2 * N_DIL_GROUPS),
        out_shape=[shape] * (2 * N_DIL_GROUPS),
        compiler_params=_cparams(("parallel",)),
        name="attn_sample",
    )(*args)
    return [(outs[2 * gi], outs[2 * gi + 1]) for gi in range(N_DIL_GROUPS)]


def _mix_out_kernel(x_ref, y_ref, gs_ref, ga_ref, o0, o1, o2, l0, l1, l2, wso_ref, wao_ref, wo_ref, out_ref,
                    *, token_lo, token_hi, seg_len):
    ssd_out = _dot(y_ref[...], wso_ref[...])
    halves = []
    for half in range(2):
        ls = [l0[half], l1[half], l2[half]]
        os_ = [o0[half], o1[half], o2[half]]
        m = jnp.maximum(jnp.maximum(ls[0], ls[1]), ls[2])
        es = [jnp.exp(l - m) for l in ls]
        den = es[0] + es[1] + es[2]
        o = (es[0] / den) * os_[0] + (es[1] / den) * os_[1] + (es[2] / den) * os_[2]
        halves.append(o.astype(BF16))
    attn_out = _dot(jnp.concatenate(halves, axis=1), wao_ref[...])
    merged = jax.nn.sigmoid(gs_ref[...]) * ssd_out + jax.nn.sigmoid(ga_ref[...]) * attn_out
    upd = _dot(merged.astype(BF16), wo_ref[...])
    if seg_len is not None:
        rows = x_ref.shape[0]
        ris = lax.broadcasted_iota(jnp.int32, (rows, 1), 0) % seg_len
        upd = jnp.where((ris >= token_lo) & (ris < token_hi), upd, 0.0)
    out_ref[...] = x_ref[...] + upd


def _mix_out(x, y, proj, attn, l, prm, tm, seg_len=None):
    n = x.shape[0]
    half_spec = pl.BlockSpec((2, tm, LANES), lambda i: (0, i, 0))
    (o0, l0), (o1, l1), (o2, l2) = attn
    return pl.pallas_call(
        functools.partial(_mix_out_kernel, token_lo=SEG_T0, token_hi=SEG_T0 + 4, seg_len=seg_len),
        grid=(n // tm,),
        in_specs=[pl.BlockSpec((tm, D_MODEL), lambda i: (i, 0)),
                  pl.BlockSpec((tm, D_INNER), lambda i: (i, 0)),
                  pl.BlockSpec((tm, D_MODEL), lambda i: (i, COL_GS // D_MODEL)),
                  pl.BlockSpec((tm, D_MODEL), lambda i: (i, COL_GA // D_MODEL)),
                  half_spec, half_spec, half_spec, half_spec, half_spec, half_spec,
                  _layer_spec(l, (D_INNER, D_MODEL), single=True),
                  _layer_spec(l, (ATTN_GROUP_WIDTH, D_MODEL), single=True),
                  _layer_spec(l, (D_MODEL, D_MODEL), single=True)],
        out_specs=pl.BlockSpec((tm, D_MODEL), lambda i: (i, 0)),
        out_shape=jax.ShapeDtypeStruct((n, D_MODEL), F32),
        compiler_params=_cparams(("parallel",)),
        name="mix_out",
    )(x, y, proj, proj, o0, o1, o2, l0, l1, l2, prm["w_ssd_out"], prm["w_attn_out"], prm["w_o"])


def _rel_bucket(dist):
    max_exact = REL_BUCKETS // 2
    d = jnp.maximum(dist, 1).astype(F32)
    large = max_exact + (jnp.log(d / max_exact) / math.log(REL_MAX_DIST / max_exact)
                         * (REL_BUCKETS - max_exact)).astype(jnp.int32)
    large = jnp.minimum(large, REL_BUCKETS - 1)
    return jnp.where(dist < max_exact, dist, large)


def _bias_lookup(tab, dist):
    onehot = (_rel_bucket(dist)[..., None] == jnp.arange(REL_BUCKETS)).astype(F32)
    return jnp.einsum("...k,kh->h...", onehot, tab.astype(F32), precision=lax.Precision.HIGHEST)


def _prompt_bias(rel_bias, gi):
    dil, reach = DILATIONS[gi], WINDOWS[gi] // DILATIONS[gi]
    blk = ATTN_BLOCK
    step = (jnp.arange(blk)[:, None] + blk) - jnp.arange(2 * blk)[None, :]
    tab = rel_bias[:, gi * HEADS_PER_GROUP:(gi + 1) * HEADS_PER_GROUP]
    bias = _bias_lookup(tab, jnp.maximum(step, 0) * dil)
    valid = (step >= 0) & (step <= reach)
    first = valid & (jnp.arange(2 * blk)[None, :] >= blk)
    neg = jnp.float32(-jnp.inf)
    return jnp.stack([jnp.where(first[None], bias, neg), jnp.where(valid[None], bias, neg)])


def _sample_bias(rel_bias, gi, row_pos):
    dil, window = DILATIONS[gi], WINDOWS[gi]
    tab = rel_bias[:, gi * HEADS_PER_GROUP:(gi + 1) * HEADS_PER_GROUP]
    seg_row = jnp.arange(SEG)
    is_tok = (seg_row >= SEG_T0) & (seg_row < SEG_T0 + 4)
    t = jnp.where(is_tok, seg_row - SEG_T0, 0)
    neg = jnp.float32(-jnp.inf)

    def table(dist, ok):
        ok = ok & (dist >= 0) & (dist % dil == 0) & (dist // dil <= window // dil)
        tbl = jnp.where(ok[None], _bias_lookup(tab, jnp.maximum(dist, 0)), neg)
        return tbl.reshape(2, 2 * SEG, dist.shape[1])

    dist_c = t[:, None] - row_pos[None, :]
    bias_c = table(dist_c, jnp.ones(dist_c.shape, bool))
    key_tok = seg_row - SEG_T0
    dist_n = t[:, None] - key_tok[None, :]
    bias_n = table(dist_n, jnp.broadcast_to(is_tok[None, :], dist_n.shape))
    return bias_c, bias_n


def _ssd_mats(rows, seg_len):
    l = jnp.arange(rows)[:, None]
    s = jnp.arange(rows)[None, :]
    same = (l // seg_len) == (s // seg_len)
    causal = same & (s <= l)
    head = jnp.arange(LANES)[:, None]
    chan = jnp.arange(D_INNER)[None, :] // SSD_HEAD_DIM
    spread = jnp.tile((head == chan).astype(BF16), (3, 1))
    return causal.astype(F32), same.astype(F32), causal.astype(F32), spread


def _kv_rows_minor(c):
    d, b, rows = c.shape[:3]
    return c.transpose(0, 1, 3, 4, 5, 2).reshape(d, b, KV_WIDTH, rows)


def kernel(x_prompt, x_sample, cache_kv_w128, cache_kv_w512, cache_kv_w2048, state_conv, state_ssm, rel_bias,
           ffn1_norm, ffn1_w13, ffn1_w2, mix_norm, w_in, conv_w, conv_b, dt_bias, a_log, d_skip, ssd_norm,
           w_ssd_out, w_attn_out, w_o, ffn2_norm, ffn2_w13, ffn2_w2, final_norm):
    batch, seq, _ = x_prompt.shape
    dec_batch, dec_seq, _ = x_sample.shape
    n_p = batch * seq
    n_s = dec_batch * SEG
    tm_p, tm_s = 512, min(512, n_s)
    tmi_p, tmi_s = 2048, min(1024, n_s)
    tok_lo, tok_hi = SEG_T0, SEG_T0 + dec_seq

    w_t = jnp.swapaxes(w_in, 1, 2)
    z_w, xbc_w, dt_w, q_w, k_w, v_w, gs_w, ga_w = jnp.split(
        w_t, [2048, 6144, 6176, 6944, 7712, 8480, 9504], axis=1)
    qkv_w = []
    for gi in range(N_DIL_GROUPS):
        sl = slice(gi * ATTN_GROUP_WIDTH, (gi + 1) * ATTN_GROUP_WIDTH)
        qkv_w += [q_w[:, sl], k_w[:, sl], v_w[:, sl]]
    pad_w = jnp.zeros((DEPTH, PROJ_W - COL_DT - SSD_HEADS, D_MODEL), w_in.dtype)
    w_in_t = jnp.concatenate([xbc_w, z_w, gs_w, ga_w] + qkv_w + [dt_w, pad_w], axis=1).astype(BF16)

    def pad_heads(v):
        return jnp.pad(v, ((0, 0), (0, LANES - SSD_HEADS)))[:, None, :]

    prm = dict(
        ffn1_norm=ffn1_norm[:, None], ffn1_w13=ffn1_w13.astype(BF16), ffn1_w2=ffn1_w2.astype(BF16),
        mix_norm=mix_norm[:, None], conv_w=conv_w, conv_b=conv_b[:, None],
        dt_bias=pad_heads(dt_bias), a_log=pad_heads(a_log),
        d_skip=jnp.repeat(d_skip, SSD_HEAD_DIM, axis=1)[:, None], ssd_norm=ssd_norm[:, None],
        w_ssd_out=w_ssd_out.astype(BF16), w_attn_out=w_attn_out.astype(BF16), w_o=w_o.astype(BF16),
        ffn2_norm=ffn2_norm[:, None], ffn2_w13=ffn2_w13.astype(BF16), ffn2_w2=ffn2_w2.astype(BF16))

    mats_p = _ssd_mats(SSD_CHUNK, SSD_CHUNK)
    rows_s = 8 * SEG
    mats_s = _ssd_mats(rows_s, SEG)
    bias_p = [_prompt_bias(rel_bias, gi) for gi in range(N_DIL_GROUPS)]
    caches = (cache_kv_w128, cache_kv_w512, cache_kv_w2048)
    bias_s, caches_t = [], []
    for gi in range(N_DIL_GROUPS):
        lb = caches[gi].shape[2]
        bias_s.append(_sample_bias(rel_bias, gi, jnp.arange(lb) - lb))
        caches_t.append(_kv_rows_minor(caches[gi]))
    conv_state_rows = jnp.pad(state_conv, ((0, 0), (0, 0), (0, SEG - (CONV_WIDTH - 1)), (0, 0))).reshape(
        DEPTH, n_s, CONV_DIM)
    sample_bt = 4

    xp = x_prompt.reshape(n_p, D_MODEL)
    xs = jnp.pad(x_sample, ((0, 0), (SEG_T0, SEG - SEG_T0 - dec_seq), (0, 0))).reshape(n_s, D_MODEL)
    ssm_p_shape = (DEPTH, batch, SSD_HEADS, SSD_HEAD_DIM, D_STATE)
    ssm_s_shape = (DEPTH, dec_batch, SSD_HEADS, SSD_HEAD_DIM, D_STATE)
    kvt_shapes = [(DEPTH, batch, KV_WIDTH, min(WINDOWS[gi], seq)) for gi in range(N_DIL_GROUPS)]
    ssm_p = ssm_s = None
    kvt_p = [None] * N_DIL_GROUPS
    fn = final_norm[None]

    kv_s = [[], [], []]
    conv_p, conv_s = [], []
    for l in range(DEPTH):
        xp = _ffn(xp, l, prm["ffn1_norm"], prm["ffn1_w13"], prm["ffn1_w2"], tm_p)
        xs = _ffn(xs, l, prm["ffn1_norm"], prm["ffn1_w13"], prm["ffn1_w2"], tm_s)
        proj_p = _inproj(xp, l, prm["mix_norm"], w_in_t, tmi_p)
        proj_s = _inproj(xs, l, prm["mix_norm"], w_in_t, tmi_s)

        y_p, ssm_p = _ssd(proj_p, l, prm, mats_p, ssm_p_shape, ssm_p, batch, seq // SSD_CHUNK, SSD_CHUNK,
                          SSD_CHUNK, 0, SSD_CHUNK)
        y_s, ssm_s = _ssd(proj_s, l, prm, mats_s, ssm_s_shape, ssm_s, n_s // rows_s, 1, rows_s, SEG,
                          tok_lo, tok_hi, conv_state=conv_state_rows, ssm_state=state_ssm)
        attn_p = []
        for gi in range(N_DIL_GROUPS):
            o, lse, kvt_p[gi] = _attn_prompt(proj_p, l, bias_p[gi], kvt_shapes[gi], kvt_p[gi], gi, batch, seq)
            attn_p.append((o, lse))
        attn_s = _attn_sample(proj_s, l, caches_t, bias_s, sample_bt)

        xp = _mix_out(xp, y_p, proj_p, attn_p, l, prm, tm_p)
        xs = _mix_out(xs, y_s, proj_s, attn_s, l, prm, tm_s, seg_len=SEG)
        last = fn if l == DEPTH - 1 else None
        xp = _ffn(xp, l, prm["ffn2_norm"], prm["ffn2_w13"], prm["ffn2_w2"], tm_p, final_gain=last)
        xs = _ffn(xs, l, prm["ffn2_norm"], prm["ffn2_w13"], prm["ffn2_w2"], tm_s, final_gain=last)

        pp = proj_p.reshape(batch, seq, PROJ_W)
        ps = proj_s.reshape(dec_batch, SEG, PROJ_W)
        conv_p.append(pp[:, seq - (CONV_WIDTH - 1):, COL_XBC:COL_XBC + CONV_DIM])
        conv_s.append(ps[:, tok_hi - (CONV_WIDTH - 1):tok_hi, COL_XBC:COL_XBC + CONV_DIM])
        for gi in range(N_DIL_GROUPS):
            c0 = COL_QKV + gi * ATTN_WIDTH + ATTN_GROUP_WIDTH
            kv_s[gi].append(ps[:, tok_lo:tok_hi, c0:c0 + KV_WIDTH].reshape(
                dec_batch, dec_seq, 2, HEADS_PER_GROUP, ATTN_HEAD_DIM))

    def kv_prompt(kvt):
        keep = kvt.shape[-1]
        return kvt.reshape(DEPTH, batch, 2, HEADS_PER_GROUP, ATTN_HEAD_DIM, keep).transpose(0, 1, 5, 2, 3, 4)

    y_prompt = xp.reshape(batch, seq, D_MODEL)
    y_sample = xs.reshape(dec_batch, SEG, D_MODEL)[:, tok_lo:tok_hi]
    return (y_prompt, y_sample,
            kv_prompt(kvt_p[0]), kv_prompt(kvt_p[1]), kv_prompt(kvt_p[2]), jnp.stack(conv_p), ssm_p,
            jnp.stack(kv_s[0]), jnp.stack(kv_s[1]), jnp.stack(kv_s[2]), jnp.stack(conv_s), ssm_s)
```

```python
import functools
import math

import jax
import jax.numpy as jnp
from jax import lax
from jax.experimental import pallas as pl
from jax.experimental.pallas import tpu as pltpu

F32 = jnp.float32
BF16 = jnp.bfloat16

D_MODEL = 1024
DEPTH = 4
D_INNER = 2048
SSD_HEAD_DIM = 64
SSD_HEADS = 32
SSD_GROUPS = 8
SSD_HPG = 4
D_STATE = 128
CONV_WIDTH = 4
CONV_DIM = 4096
SSD_CHUNK = 128
ATTN_HEAD_DIM = 64
HEADS_PER_GROUP = 4
WINDOWS = (128, 512, 2048)
DILATIONS = (1, 4, 16)
N_DIL_GROUPS = 3
ATTN_WIDTH = 768
ATTN_GROUP_WIDTH = HEADS_PER_GROUP * ATTN_HEAD_DIM
KV_WIDTH = 2 * ATTN_GROUP_WIDTH
ATTN_BLOCK = 128
ATTN_SCALE = ATTN_HEAD_DIM ** -0.5
REL_BUCKETS = 32
REL_MAX_DIST = 2048
D_FF = 2816
FFN_RES = 0.5
NORM_EPS = 1e-6
LOG2E = 1.4426950408889634

LANES = 128
SEG = 4
SEG_T0 = 0

COL_XBC = 0
COL_Z = COL_XBC + CONV_DIM
COL_GS = COL_Z + D_INNER
COL_GA = COL_GS + D_MODEL
COL_QKV = COL_GA + D_MODEL
COL_DT = COL_QKV + 3 * ATTN_WIDTH
PROJ_W = 10752
PROJ_TN = 768

VMEM_LIMIT = 56 * 1024 * 1024


def _cparams(sem):
    return pltpu.CompilerParams(dimension_semantics=sem, vmem_limit_bytes=VMEM_LIMIT)


def _layer_spec(l, shape, single=False):
    nd = len(shape)
    kw = dict(pipeline_mode=pl.Buffered(1)) if single else {}
    return pl.BlockSpec((None,) + tuple(shape), lambda *_: (l,) + (0,) * nd, **kw)


def _const_spec(shape):
    nd = len(shape)
    return pl.BlockSpec(tuple(shape), lambda *_: (0,) * nd)


_ANY = pl.BlockSpec(memory_space=pl.ANY)


def _rms(x, g):
    return x * lax.rsqrt(jnp.mean(x * x, axis=-1, keepdims=True) + NORM_EPS) * g


def _silu(x):
    return x * jax.nn.sigmoid(x)


def _dot(a, b):
    return jnp.dot(a, b, preferred_element_type=F32)


def _dot_nt(a, b):
    return lax.dot_general(a, b, (((1,), (1,)), ((), ())), preferred_element_type=F32)


def _dot_tn(a, b):
    return lax.dot_general(a, b, (((0,), (0,)), ((), ())), preferred_element_type=F32)


def _ffn_kernel(x_ref, g_ref, w13_ref, w2_ref, *rest, n_chunks):
    o_ref = rest[-1]
    x = x_ref[...]
    xn = _rms(x, g_ref[...]).astype(BF16)
    tf = D_FF // n_chunks
    acc = None
    for c in range(n_chunks):
        a = _dot(xn, w13_ref[:, c * tf:(c + 1) * tf])
        b = _dot(xn, w13_ref[:, D_FF + c * tf:D_FF + (c + 1) * tf])
        h = (_silu(a) * b).astype(BF16)
        d = _dot(h, w2_ref[c * tf:(c + 1) * tf, :])
        acc = d if acc is None else acc + d
    y = x + FFN_RES * acc
    if len(rest) == 2:
        y = _rms(y, rest[0][...])
    o_ref[...] = y


def _ffn(x, l, g, w13, w2, tm, final_gain=None):
    n = x.shape[0]
    extra = [] if final_gain is None else [final_gain]
    return pl.pallas_call(
        functools.partial(_ffn_kernel, n_chunks=2),
        grid=(n // tm,),
        in_specs=[pl.BlockSpec((tm, D_MODEL), lambda i: (i, 0)),
                  _layer_spec(l, (1, D_MODEL)),
                  _layer_spec(l, (D_MODEL, 2 * D_FF), single=True),
                  _layer_spec(l, (D_FF, D_MODEL), single=True)] + [_const_spec((1, D_MODEL))] * len(extra),
        out_specs=pl.BlockSpec((tm, D_MODEL), lambda i: (i, 0)),
        out_shape=jax.ShapeDtypeStruct((n, D_MODEL), F32),
        compiler_params=_cparams(("parallel",)),
        name="ffn",
    )(x, g, w13, w2, *extra)


def _inproj_kernel(x_ref, g_ref, w_ref, o_ref, xn_ref):
    @pl.when(pl.program_id(1) == 0)
    def _():
        xn_ref[...] = _rms(x_ref[...], g_ref[...]).astype(BF16)

    o_ref[...] = _dot_nt(xn_ref[...], w_ref[...])


def _inproj(x, l, g, w_t, tm):
    n = x.shape[0]
    return pl.pallas_call(
        _inproj_kernel,
        grid=(n // tm, PROJ_W // PROJ_TN),
        in_specs=[pl.BlockSpec((tm, D_MODEL), lambda i, j: (i, 0)),
                  _layer_spec(l, (1, D_MODEL)),
                  pl.BlockSpec((None, PROJ_TN, D_MODEL), lambda i, j: (l, j, 0))],
        out_specs=pl.BlockSpec((tm, PROJ_TN), lambda i, j: (i, j)),
        out_shape=jax.ShapeDtypeStruct((n, PROJ_W), F32),
        scratch_shapes=[pltpu.VMEM((tm, D_MODEL), BF16)],
        compiler_params=_cparams(("parallel", "arbitrary")),
        name="inproj",
    )(x, g, w_t)


def _ssd_kernel(*refs, rows, seg_len, has_state, has_prev, token_lo, token_hi):
    (xbc_ref, z_ref, dt_ref, cw_ref, cb_ref, dtb_ref, alog_ref, dskip_ref, norm_ref,
     csm_ref, totm_ref, mask_ref, spread_ref) = refs[:13]
    rest = refs[13:]
    if has_state:
        cst_ref, st_in_ref = rest[:2]
        rest = rest[2:]
    if has_prev:
        rest = rest[1:]
    y_ref, st_ref, xpad_ref, conv_ref, e2_ref, xdt_ref, w_ref = rest
    nseg = rows // seg_len
    c = pl.program_id(1)

    @pl.when(c == 0)
    def _():
        xpad_ref[:, 0:16, :] = jnp.zeros((CONV_DIM // LANES, 16, LANES), F32)
        if has_state:
            st_ref[...] = st_in_ref[...]
        else:
            st_ref[...] = jnp.zeros(st_ref.shape, F32)

    row_in_seg = lax.broadcasted_iota(jnp.int32, (rows, 1), 0) % seg_len
    xbc = xbc_ref[...]
    for s in range(CONV_DIM // LANES):
        sl = slice(s * LANES, (s + 1) * LANES)
        xs_ = xbc[:, sl]
        xpad_ref[s, pl.ds(16, rows, stride=2), :] = xs_
        conv = cb_ref[:, sl] + xs_ * cw_ref[CONV_WIDTH - 1:CONV_WIDTH, sl]
        for i in range(CONV_WIDTH - 1):
            back = CONV_WIDTH - 1 - i
            tap = xpad_ref[s, pl.ds(16 - 2 * back, rows, stride=2), :]
            if has_state:
                tap = jnp.where(row_in_seg >= back, tap, cst_ref[back - 1, :, sl])
            conv = conv + tap * cw_ref[i:i + 1, sl]
        xpad_ref[s, pl.ds(0, 8, stride=2), :] = xs_[rows - 8:rows]
        conv_ref[:, sl] = _silu(conv)

    is_token = (row_in_seg >= token_lo) & (row_in_seg < token_hi)
    dtr = dt_ref[...] + dtb_ref[...]
    dt = jnp.maximum(dtr, 0.0) + jnp.log1p(jnp.exp(-jnp.abs(dtr)))
    dt = jnp.where(is_token, dt, 0.0)
    a = dt * (-jnp.exp(alog_ref[...]))
    a_cs = jnp.dot(csm_ref[...], a, preferred_element_type=F32, precision=lax.Precision.HIGHEST)
    a_tot = jnp.dot(totm_ref[...], a, preferred_element_type=F32, precision=lax.Precision.HIGHEST)
    a_l2 = a_cs * LOG2E
    if rows < LANES:
        a_l2_sq = jnp.concatenate([a_l2, jnp.zeros((LANES - rows, LANES), F32)], axis=0)
    else:
        a_l2_sq = a_l2
    a_l2_t = a_l2_sq.T
    cdec = jnp.exp(a_tot)
    mask = mask_ref[...] > 0.5

    def split3(v):
        hi = v.astype(BF16)
        r1 = v - hi.astype(F32)
        mid = r1.astype(BF16)
        lo = (r1 - mid.astype(F32)).astype(BF16)
        return jnp.concatenate([hi, mid, lo], axis=1)

    lhs = jnp.concatenate([split3(dt), split3(dt * jnp.exp(a_tot - a_cs)), split3(jnp.exp(a_cs))], axis=0)
    for g in range(SSD_GROUPS):
        sl = slice(g * SSD_HPG * SSD_HEAD_DIM, (g + 1) * SSD_HPG * SSD_HEAD_DIM)
        spread = _dot(lhs, spread_ref[:, sl])
        xs = conv_ref[:, sl]
        xdt_ref[:, sl] = (xs * spread[0:rows]).astype(BF16)
        w_ref[:, sl] = (xs * spread[rows:2 * rows]).astype(BF16)
        e2_ref[:, sl] = spread[2 * rows:3 * rows]

    gw = SSD_HPG * SSD_HEAD_DIM
    head_of_lane = lax.broadcasted_iota(jnp.int32, (rows, gw), 1) // SSD_HEAD_DIM
    for g in range(SSD_GROUPS):
        sl = slice(g * gw, (g + 1) * gw)
        bg = conv_ref[:, D_INNER + g * D_STATE:D_INNER + (g + 1) * D_STATE].astype(BF16)
        cg = conv_ref[:, D_INNER + SSD_GROUPS * D_STATE + g * D_STATE:
                      D_INNER + SSD_GROUPS * D_STATE + (g + 1) * D_STATE].astype(BF16)
        cbm = _dot_nt(cg, bg)
        xd_g = xdt_ref[:, sl]
        m_parts, bd_parts = [], []
        for r in range(SSD_HPG):
            h = g * SSD_HPG + r
            seg = a_l2[:, h:h + 1] - a_l2_t[h:h + 1, 0:rows]
            decay = jnp.exp2(jnp.where(mask, seg, -jnp.inf))
            m_parts.append((cbm * decay).astype(BF16))
            bd_parts.append(jnp.where(head_of_lane == r, xd_g, jnp.zeros_like(xd_g)))
        y = _dot(jnp.concatenate(m_parts, axis=1), jnp.concatenate(bd_parts, axis=0))
        yo_parts = []
        for j in range(nseg):
            r0 = j * seg_len
            s_old = st_ref[j, g * SSD_HPG:(g + 1) * SSD_HPG].reshape(gw, D_STATE)
            yo_parts.append(_dot_nt(cg[r0:r0 + seg_len], s_old.astype(BF16)))
            dec = jnp.concatenate(
                [jnp.broadcast_to(cdec[r0:r0 + 1, g * SSD_HPG + r:g * SSD_HPG + r + 1], (SSD_HEAD_DIM, D_STATE))
                 for r in range(SSD_HPG)], axis=0)
            s_new = s_old * dec + _dot_tn(w_ref[r0:r0 + seg_len, sl], bg[r0:r0 + seg_len])
            st_ref[j, g * SSD_HPG:(g + 1) * SSD_HPG] = s_new.reshape(SSD_HPG, SSD_HEAD_DIM, D_STATE)
        yo = yo_parts[0] if nseg == 1 else jnp.concatenate(yo_parts, axis=0)
        y = y + yo * e2_ref[:, sl]
        y = y + dskip_ref[:, sl] * conv_ref[:, sl]
        y = y * _silu(z_ref[:, sl])
        y = y * lax.rsqrt(jnp.mean(y * y, axis=-1, keepdims=True) + NORM_EPS) * norm_ref[:, sl]
        y_ref[:, sl] = y.astype(BF16)


def _ssd(proj, l, prm, mats, st_shape, st_prev, n_outer, n_inner, rows, seg_len, token_lo, token_hi,
         conv_state=None, ssm_state=None):
    has_state = ssm_state is not None
    nseg = rows // seg_len
    n = proj.shape[0]
    st_block = (None, nseg, SSD_HEADS, SSD_HEAD_DIM, D_STATE)

    def rowmap(cb):
        return lambda o, c: (o * n_inner + c, cb)

    def st_map(o, c):
        return (l, o, 0, 0, 0)

    in_specs = [pl.BlockSpec((rows, CONV_DIM), rowmap(COL_XBC // CONV_DIM)),
                pl.BlockSpec((rows, D_INNER), rowmap(COL_Z // D_INNER)),
                pl.BlockSpec((rows, LANES), rowmap(COL_DT // LANES)),
                _layer_spec(l, (CONV_WIDTH, CONV_DIM)), _layer_spec(l, (1, CONV_DIM)),
                _layer_spec(l, (1, LANES)), _layer_spec(l, (1, LANES)),
                _layer_spec(l, (1, D_INNER)), _layer_spec(l, (1, D_INNER)),
                _const_spec((rows, rows)), _const_spec((rows, rows)), _const_spec((rows, rows)),
                _const_spec((3 * LANES, D_INNER))]
    args = [proj, proj, proj, prm["conv_w"], prm["conv_b"], prm["dt_bias"], prm["a_log"],
            prm["d_skip"], prm["ssd_norm"], mats[0], mats[1], mats[2], mats[3]]
    if has_state:
        in_specs += [pl.BlockSpec((None, CONV_WIDTH - 1, rows, CONV_DIM), lambda o, c: (l, 0, o * n_inner + c, 0)),
                     pl.BlockSpec(st_block, st_map)]
        args += [conv_state, ssm_state]
    aliases = {}
    if st_prev is not None:
        in_specs.append(_ANY)
        args.append(st_prev)
        aliases = {len(args) - 1: 1}
    y, st = pl.pallas_call(
        functools.partial(_ssd_kernel, rows=rows, seg_len=seg_len, has_state=has_state,
                          has_prev=st_prev is not None, token_lo=token_lo, token_hi=token_hi),
        grid=(n_outer, n_inner),
        in_specs=in_specs,
        out_specs=[pl.BlockSpec((rows, D_INNER), rowmap(0)), pl.BlockSpec(st_block, st_map)],
        out_shape=[jax.ShapeDtypeStruct((n, D_INNER), BF16),
                   jax.ShapeDtypeStruct(st_shape, F32)],
        scratch_shapes=[pltpu.VMEM((CONV_DIM // LANES, 16 + 2 * rows, LANES), F32),
                        pltpu.VMEM((rows, CONV_DIM), F32),
                        pltpu.VMEM((rows, D_INNER), F32),
                        pltpu.VMEM((rows, D_INNER), BF16),
                        pltpu.VMEM((rows, D_INNER), BF16)],
        input_output_aliases=aliases,
        compiler_params=_cparams(("parallel", "arbitrary")),
        name="ssd_state" if has_state else "ssd_prompt",
    )(*args)
    return y, st


def _attn_prompt_kernel(q0, q1, k0, k1, v0, v1, bias_ref, *rest, dil, seq, keep):
    o_ref, lse_ref, kvt_ref = rest[-3:]
    nblk = seq // (ATTN_BLOCK * dil)
    lo_lanes = lax.broadcasted_iota(jnp.int32, (ATTN_BLOCK, LANES), 1) < ATTN_HEAD_DIM

    tc = min(keep, 512)
    for half, (k, v) in enumerate(((k0, v0), (k1, v1))):
        for c0 in range(0, keep, tc):
            src = pl.ds(seq - keep + c0, tc)
            kvt_ref[half * LANES:(half + 1) * LANES, c0:c0 + tc] = k[src, :].T
            kvt_ref[ATTN_GROUP_WIDTH + half * LANES:ATTN_GROUP_WIDTH + (half + 1) * LANES, c0:c0 + tc] = v[src, :].T

    def rows(start):
        if dil == 1:
            return pl.ds(pl.multiple_of(start, ATTN_BLOCK), ATTN_BLOCK)
        return pl.ds(start, ATTN_BLOCK, stride=dil)

    def body(i, carry):
        r = i // nblk
        blk = i % nblk
        cur = rows(r + blk * ATTN_BLOCK * dil)
        prev = rows(r + jnp.maximum(blk - 1, 0) * ATTN_BLOCK * dil)
        tab = jnp.minimum(blk, 1)
        for half, (q, k, v) in enumerate(((q0, k0, v0), (q1, k1, v1))):
            qb = q[cur, :] * ATTN_SCALE
            kk = jnp.concatenate([k[prev, :], k[cur, :]], axis=0).astype(BF16)
            vv = jnp.concatenate([v[prev, :], v[cur, :]], axis=0).astype(BF16)
            outs, lses = [], []
            for hh in range(2):
                lanes = lo_lanes if hh == 0 else jnp.logical_not(lo_lanes)
                qm = jnp.where(lanes, qb, 0.0).astype(BF16)
                s = _dot_nt(qm, kk) + bias_ref[tab, 2 * half + hh]
                m = jnp.max(s, axis=-1, keepdims=True)
                p = jnp.exp(s - m)
                l = jnp.sum(p, axis=-1, keepdims=True)
                outs.append(_dot(p.astype(BF16), vv) / l)
                lses.append(jnp.broadcast_to(m + jnp.log(l), (ATTN_BLOCK, LANES)))
            o_ref[half, cur, :] = jnp.where(lo_lanes, outs[0], outs[1])
            lse_ref[half, cur, :] = jnp.where(lo_lanes, lses[0], lses[1])
        return carry

    lax.fori_loop(0, dil * nblk, body, 0, unroll=4)


def _attn_prompt(proj, l, bias, kvt_shape, kvt_prev, gi, batch, seq):
    dil = DILATIONS[gi]
    keep = kvt_shape[-1]
    base = (COL_QKV + gi * ATTN_WIDTH) // LANES

    def col(j):
        return pl.BlockSpec((seq, LANES), lambda b: (b, base + j))

    out_spec = pl.BlockSpec((2, seq, LANES), lambda b: (0, b, 0))
    shape = jax.ShapeDtypeStruct((2, batch * seq, LANES), F32)
    prev = [] if kvt_prev is None else [kvt_prev]
    return pl.pallas_call(
        functools.partial(_attn_prompt_kernel, dil=dil, seq=seq, keep=keep),
        grid=(batch,),
        in_specs=[col(0), col(1), col(2), col(3), col(4), col(5),
                  _const_spec((2, HEADS_PER_GROUP, ATTN_BLOCK, 2 * ATTN_BLOCK))] + ([_ANY] if prev else []),
        out_specs=[out_spec, out_spec,
                   pl.BlockSpec((None, None, KV_WIDTH, keep), lambda b: (l, b, 0, 0))],
        out_shape=[shape, shape, jax.ShapeDtypeStruct(kvt_shape, F32)],
        input_output_aliases={7: 2} if prev else {},
        compiler_params=_cparams(("parallel",)),
        name=f"attn_prompt_d{dil}",
    )(proj, proj, proj, proj, proj, proj, bias, *prev)


def _attn_sample_kernel(*refs, bt):
    n_in = 9 * N_DIL_GROUPS
    ins, outs = refs[:n_in], refs[n_in:]
    lo_seg = lax.broadcasted_iota(jnp.int32, (SEG, LANES), 1) < ATTN_HEAD_DIM
    lo_pair = lax.broadcasted_iota(jnp.int32, (2 * SEG, LANES), 1) < ATTN_HEAD_DIM
    top_pair = lax.broadcasted_iota(jnp.int32, (2 * SEG, LANES), 0) < SEG
    head_lanes = lo_pair == top_pair

    for b in range(bt):
        seg = pl.ds(b * SEG, SEG)
        for gi in range(N_DIL_GROUPS):
            q0, q1, kn0, kn1, vn0, vn1, cache_ref, bias_c_ref, bias_n_ref = ins[9 * gi:9 * gi + 9]
            o_ref, lse_ref = outs[2 * gi:2 * gi + 2]
            for half, (q, kn, vn) in enumerate(((q0, kn0, vn0), (q1, kn1, vn1))):
                qb = q[seg, :] * ATTN_SCALE
                q_pair = jnp.where(head_lanes, jnp.concatenate([qb, qb], axis=0), 0.0).astype(BF16)
                knew = kn[seg, :].astype(BF16)
                vnew = vn[seg, :].astype(BF16)
                kt = cache_ref[b, half * LANES:(half + 1) * LANES, :].astype(BF16)
                vt = cache_ref[b, ATTN_GROUP_WIDTH + half * LANES:
                               ATTN_GROUP_WIDTH + (half + 1) * LANES, :].astype(BF16)
                sc = _dot(q_pair, kt) + bias_c_ref[half]
                sn = _dot_nt(q_pair, knew) + bias_n_ref[half]
                m = jnp.maximum(jnp.max(sc, axis=-1, keepdims=True), jnp.max(sn, axis=-1, keepdims=True))
                pc = jnp.exp(sc - m)
                pn = jnp.exp(sn - m)
                l = jnp.sum(pc, axis=-1, keepdims=True) + jnp.sum(pn, axis=-1, keepdims=True)
                o = (_dot_nt(pc.astype(BF16), vt) + _dot(pn.astype(BF16), vnew)) / l
                lse = jnp.broadcast_to(m + jnp.log(l), (2 * SEG, LANES))
                o_ref[half, seg, :] = jnp.where(lo_seg, o[0:SEG], o[SEG:2 * SEG])
                lse_ref[half, seg, :] = jnp.where(lo_seg, lse[0:SEG], lse[SEG:2 * SEG])


def _attn_sample(proj, l, caches_t, biases, bt):
    nb = caches_t[0].shape[1]
    in_specs, args = [], []
    for gi in range(N_DIL_GROUPS):
        rows_c = caches_t[gi].shape[3]
        base = (COL_QKV + gi * ATTN_WIDTH) // LANES
        in_specs += [pl.BlockSpec((bt * SEG, LANES), lambda i, c=base + j: (i, c)) for j in range(6)]
        in_specs += [pl.BlockSpec((None, bt, KV_WIDTH, rows_c), lambda i: (l, i, 0, 0)),
                     _const_spec((2, 2 * SEG, rows_c)), _const_spec((2, 2 * SEG, SEG))]
        args += [proj] * 6 + [caches_t[gi], biases[gi][0], biases[gi][1]]
    out_spec = pl.BlockSpec((2, bt * SEG, LANES), lambda i: (0, i, 0))
    shape = jax.ShapeDtypeStruct((2, nb * SEG, LANES), F32)
    outs = pl.pallas_call(
        functools.partial(_attn_sample_kernel, bt=bt),
        grid=(nb // bt,),
        in_specs=in_specs,
        out_specs=[out_spec] * (2 * N_DIL_GROUPS),
        out_shape=[shape] * (2 * N_DIL_GROUPS),
        compiler_params=_cparams(("parallel",)),
        name="attn_sample",
    )(*args)
    return [(outs[2 * gi], outs[2 * gi + 1]) for gi in range(N_DIL_GROUPS)]


def _mix_out_kernel(x_ref, y_ref, gs_ref, ga_ref, o0, o1, o2, l0, l1, l2, wso_ref, wao_ref, wo_ref, out_ref,
                    *, token_lo, token_hi, seg_len):
    ssd_out = _dot(y_ref[...], wso_ref[...])
    halves = []
    for half in range(2):
        ls = [l0[half], l1[half], l2[half]]
        os_ = [o0[half], o1[half], o2[half]]
        m = jnp.maximum(jnp.maximum(ls[0], ls[1]), ls[2])
        es = [jnp.exp(l - m) for l in ls]
        den = es[0] + es[1] + es[2]
        o = (es[0] / den) * os_[0] + (es[1] / den) * os_[1] + (es[2] / den) * os_[2]
        halves.append(o.astype(BF16))
    attn_out = _dot(jnp.concatenate(halves, axis=1), wao_ref[...])
    merged = jax.nn.sigmoid(gs_ref[...]) * ssd_out + jax.nn.sigmoid(ga_ref[...]) * attn_out
    upd = _dot(merged.astype(BF16), wo_ref[...])
    if seg_len is not None:
        rows = x_ref.shape[0]
        ris = lax.broadcasted_iota(jnp.int32, (rows, 1), 0) % seg_len
        upd = jnp.where((ris >= token_lo) & (ris < token_hi), upd, 0.0)
    out_ref[...] = x_ref[...] + upd


def _mix_out(x, y, proj, attn, l, prm, tm, seg_len=None):
    n = x.shape[0]
    half_spec = pl.BlockSpec((2, tm, LANES), lambda i: (0, i, 0))
    (o0, l0), (o1, l1), (o2, l2) = attn
    return pl.pallas_call(
        functools.partial(_mix_out_kernel, token_lo=SEG_T0, token_hi=SEG_T0 + 4, seg_len=seg_len),
        grid=(n // tm,),
        in_specs=[pl.BlockSpec((tm, D_MODEL), lambda i: (i, 0)),
                  pl.BlockSpec((tm, D_INNER), lambda i: (i, 0)),
                  pl.BlockSpec((tm, D_MODEL), lambda i: (i, COL_GS // D_MODEL)),
                  pl.BlockSpec((tm, D_MODEL), lambda i: (i, COL_GA // D_MODEL)),
                  half_spec, half_spec, half_spec, half_spec, half_spec, half_spec,
                  _layer_spec(l, (D_INNER, D_MODEL), single=True),
                  _layer_spec(l, (ATTN_GROUP_WIDTH, D_MODEL), single=True),
                  _layer_spec(l, (D_MODEL, D_MODEL), single=True)],
        out_specs=pl.BlockSpec((tm, D_MODEL), lambda i: (i, 0)),
        out_shape=jax.ShapeDtypeStruct((n, D_MODEL), F32),
        compiler_params=_cparams(("parallel",)),
        name="mix_out",
    )(x, y, proj, proj, o0, o1, o2, l0, l1, l2, prm["w_ssd_out"], prm["w_attn_out"], prm["w_o"])


def _rel_bucket(dist):
    max_exact = REL_BUCKETS // 2
    d = jnp.maximum(dist, 1).astype(F32)
    large = max_exact + (jnp.log(d / max_exact) / math.log(REL_MAX_DIST / max_exact)
                         * (REL_BUCKETS - max_exact)).astype(jnp.int32)
    large = jnp.minimum(large, REL_BUCKETS - 1)
    return jnp.where(dist < max_exact, dist, large)


def _bias_lookup(tab, dist):
    onehot = (_rel_bucket(dist)[..., None] == jnp.arange(REL_BUCKETS)).astype(F32)
    return jnp.einsum("...k,kh->h...", onehot, tab.astype(F32), precision=lax.Precision.HIGHEST)


def _prompt_bias(rel_bias, gi):
    dil, reach = DILATIONS[gi], WINDOWS[gi] // DILATIONS[gi]
    blk = ATTN_BLOCK
    step = (jnp.arange(blk)[:, None] + blk) - jnp.arange(2 * blk)[None, :]
    tab = rel_bias[:, gi * HEADS_PER_GROUP:(gi + 1) * HEADS_PER_GROUP]
    bias = _bias_lookup(tab, jnp.maximum(step, 0) * dil)
    valid = (step >= 0) & (step <= reach)
    first = valid & (jnp.arange(2 * blk)[None, :] >= blk)
    neg = jnp.float32(-jnp.inf)
    return jnp.stack([jnp.where(first[None], bias, neg), jnp.where(valid[None], bias, neg)])


def _sample_bias(rel_bias, gi, row_pos):
    dil, window = DILATIONS[gi], WINDOWS[gi]
    tab = rel_bias[:, gi * HEADS_PER_GROUP:(gi + 1) * HEADS_PER_GROUP]
    seg_row = jnp.arange(SEG)
    is_tok = (seg_row >= SEG_T0) & (seg_row < SEG_T0 + 4)
    t = jnp.where(is_tok, seg_row - SEG_T0, 0)
    neg = jnp.float32(-jnp.inf)

    def table(dist, ok):
        ok = ok & (dist >= 0) & (dist % dil == 0) & (dist // dil <= window // dil)
        tbl = jnp.where(ok[None], _bias_lookup(tab, jnp.maximum(dist, 0)), neg)
        return tbl.reshape(2, 2 * SEG, dist.shape[1])

    dist_c = t[:, None] - row_pos[None, :]
    bias_c = table(dist_c, jnp.ones(dist_c.shape, bool))
    key_tok = seg_row - SEG_T0
    dist_n = t[:, None] - key_tok[None, :]
    bias_n = table(dist_n, jnp.broadcast_to(is_tok[None, :], dist_n.shape))
    return bias_c, bias_n


def _ssd_mats(rows, seg_len):
    l = jnp.arange(rows)[:, None]
    s = jnp.arange(rows)[None, :]
    same = (l // seg_len) == (s // seg_len)
    causal = same & (s <= l)
    head = jnp.arange(LANES)[:, None]
    chan = jnp.arange(D_INNER)[None, :] // SSD_HEAD_DIM
    spread = jnp.tile((head == chan).astype(BF16), (3, 1))
    return causal.astype(F32), same.astype(F32), causal.astype(F32), spread


def _kv_rows_minor(c):
    d, b, rows = c.shape[:3]
    return c.transpose(0, 1, 3, 4, 5, 2).reshape(d, b, KV_WIDTH, rows)


def kernel(x_prompt, x_sample, cache_kv_w128, cache_kv_w512, cache_kv_w2048, state_conv, state_ssm, rel_bias,
           ffn1_norm, ffn1_w13, ffn1_w2, mix_norm, w_in, conv_w, conv_b, dt_bias, a_log, d_skip, ssd_norm,
           w_ssd_out, w_attn_out, w_o, ffn2_norm, ffn2_w13, ffn2_w2, final_norm):
    batch, seq, _ = x_prompt.shape
    dec_batch, dec_seq, _ = x_sample.shape
    n_p = batch * seq
    n_s = dec_batch * SEG
    tm_p, tm_s = 512, min(512, n_s)
    tmi_p, tmi_s = 2048, min(1024, n_s)
    tok_lo, tok_hi = SEG_T0, SEG_T0 + dec_seq

    w_t = jnp.swapaxes(w_in, 1, 2)
    z_w, xbc_w, dt_w, q_w, k_w, v_w, gs_w, ga_w = jnp.split(
        w_t, [2048, 6144, 6176, 6944, 7712, 8480, 9504], axis=1)
    qkv_w = []
    for gi in range(N_DIL_GROUPS):
        sl = slice(gi * ATTN_GROUP_WIDTH, (gi + 1) * ATTN_GROUP_WIDTH)
        qkv_w += [q_w[:, sl], k_w[:, sl], v_w[:, sl]]
    pad_w = jnp.zeros((DEPTH, PROJ_W - COL_DT - SSD_HEADS, D_MODEL), w_in.dtype)
    w_in_t = jnp.concatenate([xbc_w, z_w, gs_w, ga_w] + qkv_w + [dt_w, pad_w], axis=1).astype(BF16)

    def pad_heads(v):
        return jnp.pad(v, ((0, 0), (0, LANES - SSD_HEADS)))[:, None, :]

    prm = dict(
        ffn1_norm=ffn1_norm[:, None], ffn1_w13=ffn1_w13.astype(BF16), ffn1_w2=ffn1_w2.astype(BF16),
        mix_norm=mix_norm[:, None], conv_w=conv_w, conv_b=conv_b[:, None],
        dt_bias=pad_heads(dt_bias), a_log=pad_heads(a_log),
        d_skip=jnp.repeat(d_skip, SSD_HEAD_DIM, axis=1)[:, None], ssd_norm=ssd_norm[:, None],
        w_ssd_out=w_ssd_out.astype(BF16), w_attn_out=w_attn_out.astype(BF16), w_o=w_o.astype(BF16),
        ffn2_norm=ffn2_norm[:, None], ffn2_w13=ffn2_w13.astype(BF16), ffn2_w2=ffn2_w2.astype(BF16))

    mats_p = _ssd_mats(SSD_CHUNK, SSD_CHUNK)
    rows_s = 8 * SEG
    mats_s = _ssd_mats(rows_s, SEG)
    bias_p = [_prompt_bias(rel_bias, gi) for gi in range(N_DIL_GROUPS)]
    caches = (cache_kv_w128, cache_kv_w512, cache_kv_w2048)
    bias_s, caches_t = [], []
    for gi in range(N_DIL_GROUPS):
        lb = caches[gi].shape[2]
        bias_s.append(_sample_bias(rel_bias, gi, jnp.arange(lb) - lb))
        caches_t.append(_kv_rows_minor(caches[gi]))
    conv_state_rows = jnp.stack(
        [jnp.pad(state_conv[:, :, CONV_WIDTH - 1 - k:, :], ((0, 0), (0, 0), (0, SEG - k), (0, 0))).reshape(
            DEPTH, n_s, CONV_DIM) for k in range(1, CONV_WIDTH)], axis=1)
    sample_bt = 4

    xp = x_prompt.reshape(n_p, D_MODEL)
    xs = jnp.pad(x_sample, ((0, 0), (SEG_T0, SEG - SEG_T0 - dec_seq), (0, 0))).reshape(n_s, D_MODEL)
    ssm_p_shape = (DEPTH, batch, SSD_HEADS, SSD_HEAD_DIM, D_STATE)
    ssm_s_shape = (DEPTH, dec_batch, SSD_HEADS, SSD_HEAD_DIM, D_STATE)
    kvt_shapes = [(DEPTH, batch, KV_WIDTH, min(WINDOWS[gi], seq)) for gi in range(N_DIL_GROUPS)]
    ssm_p = ssm_s = None
    kvt_p = [None] * N_DIL_GROUPS
    fn = final_norm[None]

    kv_s = [[], [], []]
    conv_p, conv_s = [], []
    for l in range(DEPTH):
        xp = _ffn(xp, l, prm["ffn1_norm"], prm["ffn1_w13"], prm["ffn1_w2"], tm_p)
        xs = _ffn(xs, l, prm["ffn1_norm"], prm["ffn1_w13"], prm["ffn1_w2"], tm_s)
        proj_p = _inproj(xp, l, prm["mix_norm"], w_in_t, tmi_p)
        proj_s = _inproj(xs, l, prm["mix_norm"], w_in_t, tmi_s)

        y_p, ssm_p = _ssd(proj_p, l, prm, mats_p, ssm_p_shape, ssm_p, batch, seq // SSD_CHUNK, SSD_CHUNK,
                          SSD_CHUNK, 0, SSD_CHUNK)
        y_s, ssm_s = _ssd(proj_s, l, prm, mats_s, ssm_s_shape, ssm_s, n_s // rows_s, 1, rows_s, SEG,
                          tok_lo, tok_hi, conv_state=conv_state_rows, ssm_state=state_ssm)
        attn_p = []
        for gi in range(N_DIL_GROUPS):
            o, lse, kvt_p[gi] = _attn_prompt(proj_p, l, bias_p[gi], kvt_shapes[gi], kvt_p[gi], gi, batch, seq)
            attn_p.append((o, lse))
        attn_s = _attn_sample(proj_s, l, caches_t, bias_s, sample_bt)

        xp = _mix_out(xp, y_p, proj_p, attn_p, l, prm, tm_p)
        xs = _mix_out(xs, y_s, proj_s, attn_s, l, prm, tm_s, seg_len=SEG)
        last = fn if l == DEPTH - 1 else None
        xp = _ffn(xp, l, prm["ffn2_norm"], prm["ffn2_w13"], prm["ffn2_w2"], tm_p, final_gain=last)
        xs = _ffn(xs, l, prm["ffn2_norm"], prm["ffn2_w13"], prm["ffn2_w2"], tm_s, final_gain=last)

        pp = proj_p.reshape(batch, seq, PROJ_W)
        ps = proj_s.reshape(dec_batch, SEG, PROJ_W)
        conv_p.append(pp[:, seq - (CONV_WIDTH - 1):, COL_XBC:COL_XBC + CONV_DIM])
        conv_s.append(ps[:, tok_hi - (CONV_WIDTH - 1):tok_hi, COL_XBC:COL_XBC + CONV_DIM])
        for gi in range(N_DIL_GROUPS):
            c0 = COL_QKV + gi * ATTN_WIDTH + ATTN_GROUP_WIDTH
            kv_s[gi].append(ps[:, tok_lo:tok_hi, c0:c0 + KV_WIDTH].reshape(
                dec_batch, dec_seq, 2, HEADS_PER_GROUP, ATTN_HEAD_DIM))

    def kv_prompt(kvt):
        keep = kvt.shape[-1]
        return kvt.reshape(DEPTH, batch, 2, HEADS_PER_GROUP, ATTN_HEAD_DIM, keep).transpose(0, 1, 5, 2, 3, 4)

    y_prompt = xp.reshape(batch, seq, D_MODEL)
    y_sample = xs.reshape(dec_batch, SEG, D_MODEL)[:, tok_lo:tok_hi]
    return (y_prompt, y_sample,
            kv_prompt(kvt_p[0]), kv_prompt(kvt_p[1]), kv_prompt(kvt_p[2]), jnp.stack(conv_p), ssm_p,
            jnp.stack(kv_s[0]), jnp.stack(kv_s[1]), jnp.stack(kv_s[2]), jnp.stack(conv_s), ssm_s)
```

```python
import functools
import math

import jax
import jax.numpy as jnp
from jax import lax
from jax.experimental import pallas as pl
from jax.experimental.pallas import tpu as pltpu

F32 = jnp.float32
BF16 = jnp.bfloat16

D_MODEL = 1024
DEPTH = 4
D_INNER = 2048
SSD_HEAD_DIM = 64
SSD_HEADS = 32
SSD_GROUPS = 8
SSD_HPG = 4
D_STATE = 128
CONV_WIDTH = 4
CONV_DIM = 4096
SSD_CHUNK = 128
ATTN_HEAD_DIM = 64
HEADS_PER_GROUP = 4
WINDOWS = (128, 512, 2048)
DILATIONS = (1, 4, 16)
N_DIL_GROUPS = 3
ATTN_WIDTH = 768
ATTN_GROUP_WIDTH = HEADS_PER_GROUP * ATTN_HEAD_DIM
KV_WIDTH = 2 * ATTN_GROUP_WIDTH
ATTN_BLOCK = 128
ATTN_SCALE = ATTN_HEAD_DIM ** -0.5
REL_BUCKETS = 32
REL_MAX_DIST = 2048
D_FF = 2816
FFN_RES = 0.5
NORM_EPS = 1e-6
LOG2E = 1.4426950408889634

LANES = 128
SEG = 4
SEG_T0 = 0

COL_XBC = 0
COL_Z = COL_XBC + CONV_DIM
COL_GS = COL_Z + D_INNER
COL_GA = COL_GS + D_MODEL
COL_QKV = COL_GA + D_MODEL
COL_DT = COL_QKV + 3 * ATTN_WIDTH
PROJ_W = 10752
PROJ_TN = 768

VMEM_LIMIT = 56 * 1024 * 1024


def _cparams(sem):
    return pltpu.CompilerParams(dimension_semantics=sem, vmem_limit_bytes=VMEM_LIMIT)


def _layer_spec(l, shape, single=False):
    nd = len(shape)
    kw = dict(pipeline_mode=pl.Buffered(1)) if single else {}
    return pl.BlockSpec((None,) + tuple(shape), lambda *_: (l,) + (0,) * nd, **kw)


def _const_spec(shape):
    nd = len(shape)
    return pl.BlockSpec(tuple(shape), lambda *_: (0,) * nd)


_ANY = pl.BlockSpec(memory_space=pl.ANY)


def _rms(x, g):
    return x * lax.rsqrt(jnp.mean(x * x, axis=-1, keepdims=True) + NORM_EPS) * g


def _silu(x):
    return x * jax.nn.sigmoid(x)


def _dot(a, b):
    return jnp.dot(a, b, preferred_element_type=F32)


def _dot_nt(a, b):
    return lax.dot_general(a, b, (((1,), (1,)), ((), ())), preferred_element_type=F32)


def _dot_tn(a, b):
    return lax.dot_general(a, b, (((0,), (0,)), ((), ())), preferred_element_type=F32)


def _ffn_kernel(x_ref, g_ref, w13_ref, w2_ref, *rest, n_chunks):
    o_ref = rest[-1]
    x = x_ref[...]
    xn = _rms(x, g_ref[...]).astype(BF16)
    tf = D_FF // n_chunks
    acc = None
    for c in range(n_chunks):
        a = _dot(xn, w13_ref[:, c * tf:(c + 1) * tf])
        b = _dot(xn, w13_ref[:, D_FF + c * tf:D_FF + (c + 1) * tf])
        h = (_silu(a) * b).astype(BF16)
        d = _dot(h, w2_ref[c * tf:(c + 1) * tf, :])
        acc = d if acc is None else acc + d
    y = x + FFN_RES * acc
    if len(rest) == 2:
        y = _rms(y, rest[0][...])
    o_ref[...] = y


def _ffn(x, l, g, w13, w2, tm, final_gain=None):
    n = x.shape[0]
    extra = [] if final_gain is None else [final_gain]
    return pl.pallas_call(
        functools.partial(_ffn_kernel, n_chunks=2),
        grid=(n // tm,),
        in_specs=[pl.BlockSpec((tm, D_MODEL), lambda i: (i, 0)),
                  _layer_spec(l, (1, D_MODEL)),
                  _layer_spec(l, (D_MODEL, 2 * D_FF), single=True),
                  _layer_spec(l, (D_FF, D_MODEL), single=True)] + [_const_spec((1, D_MODEL))] * len(extra),
        out_specs=pl.BlockSpec((tm, D_MODEL), lambda i: (i, 0)),
        out_shape=jax.ShapeDtypeStruct((n, D_MODEL), F32),
        compiler_params=_cparams(("parallel",)),
        name="ffn",
    )(x, g, w13, w2, *extra)


def _inproj_kernel(x_ref, g_ref, w_ref, o_ref, xn_ref):
    @pl.when(pl.program_id(1) == 0)
    def _():
        xn_ref[...] = _rms(x_ref[...], g_ref[...]).astype(BF16)

    o_ref[...] = _dot_nt(xn_ref[...], w_ref[...])


def _inproj(x, l, g, w_t, tm):
    n = x.shape[0]
    return pl.pallas_call(
        _inproj_kernel,
        grid=(n // tm, PROJ_W // PROJ_TN),
        in_specs=[pl.BlockSpec((tm, D_MODEL), lambda i, j: (i, 0)),
                  _layer_spec(l, (1, D_MODEL)),
                  pl.BlockSpec((None, PROJ_TN, D_MODEL), lambda i, j: (l, j, 0))],
        out_specs=pl.BlockSpec((tm, PROJ_TN), lambda i, j: (i, j)),
        out_shape=jax.ShapeDtypeStruct((n, PROJ_W), F32),
        scratch_shapes=[pltpu.VMEM((tm, D_MODEL), BF16)],
        compiler_params=_cparams(("parallel", "arbitrary")),
        name="inproj",
    )(x, g, w_t)


def _ssd_kernel(*refs, rows, seg_len, has_state, has_prev, token_lo, token_hi):
    (xbc_ref, z_ref, dt_ref, cw_ref, cb_ref, dtb_ref, alog_ref, dskip_ref, norm_ref,
     csm_ref, totm_ref, mask_ref, spread_ref) = refs[:13]
    rest = refs[13:]
    if has_state:
        cst_ref, st_in_ref = rest[:2]
        rest = rest[2:]
    if has_prev:
        rest = rest[1:]
    y_ref, st_ref, xpad_ref, conv_ref, e2_ref, xdt_ref, w_ref = rest
    nseg = rows // seg_len
    c = pl.program_id(1)

    @pl.when(c == 0)
    def _():
        xpad_ref[:, 0:16, :] = jnp.zeros((CONV_DIM // LANES, 16, LANES), F32)
        if has_state:
            st_ref[...] = st_in_ref[...]
        else:
            st_ref[...] = jnp.zeros(st_ref.shape, F32)

    row_in_seg = lax.broadcasted_iota(jnp.int32, (rows, 1), 0) % seg_len
    xbc = xbc_ref[...]
    for s in range(CONV_DIM // LANES):
        sl = slice(s * LANES, (s + 1) * LANES)
        xs_ = xbc[:, sl]
        xpad_ref[s, pl.ds(16, rows, stride=2), :] = xs_
        conv = cb_ref[:, sl] + xs_ * cw_ref[CONV_WIDTH - 1:CONV_WIDTH, sl]
        for i in range(CONV_WIDTH - 1):
            back = CONV_WIDTH - 1 - i
            tap = xpad_ref[s, pl.ds(16 - 2 * back, rows, stride=2), :]
            if has_state:
                tap = jnp.where(row_in_seg >= back, tap, cst_ref[back - 1, :, sl])
            conv = conv + tap * cw_ref[i:i + 1, sl]
        xpad_ref[s, pl.ds(0, 8, stride=2), :] = xs_[rows - 8:rows]
        conv_ref[:, sl] = _silu(conv)

    is_token = (row_in_seg >= token_lo) & (row_in_seg < token_hi)
    dtr = dt_ref[...] + dtb_ref[...]
    dt = jnp.maximum(dtr, 0.0) + jnp.log1p(jnp.exp(-jnp.abs(dtr)))
    dt = jnp.where(is_token, dt, 0.0)
    a = dt * (-jnp.exp(alog_ref[...]))
    a_cs = jnp.dot(csm_ref[...], a, preferred_element_type=F32, precision=lax.Precision.HIGHEST)
    a_tot = jnp.dot(totm_ref[...], a, preferred_element_type=F32, precision=lax.Precision.HIGHEST)
    a_l2 = a_cs * LOG2E
    if rows < LANES:
        a_l2_sq = jnp.concatenate([a_l2, jnp.zeros((LANES - rows, LANES), F32)], axis=0)
    else:
        a_l2_sq = a_l2
    a_l2_t = a_l2_sq.T
    cdec = jnp.exp(a_tot)
    mask = mask_ref[...] > 0.5

    def split3(v):
        hi = v.astype(BF16)
        r1 = v - hi.astype(F32)
        mid = r1.astype(BF16)
        lo = (r1 - mid.astype(F32)).astype(BF16)
        return jnp.concatenate([hi, mid, lo], axis=1)

    lhs = jnp.concatenate([split3(dt), split3(dt * jnp.exp(a_tot - a_cs)), split3(jnp.exp(a_cs))], axis=0)
    for g in range(SSD_GROUPS):
        sl = slice(g * SSD_HPG * SSD_HEAD_DIM, (g + 1) * SSD_HPG * SSD_HEAD_DIM)
        spread = _dot(lhs, spread_ref[:, sl])
        xs = conv_ref[:, sl]
        xdt_ref[:, sl] = (xs * spread[0:rows]).astype(BF16)
        w_ref[:, sl] = (xs * spread[rows:2 * rows]).astype(BF16)
        e2_ref[:, sl] = spread[2 * rows:3 * rows]

    gw = SSD_HPG * SSD_HEAD_DIM
    head_of_lane = lax.broadcasted_iota(jnp.int32, (rows, gw), 1) // SSD_HEAD_DIM
    for g in range(SSD_GROUPS):
        sl = slice(g * gw, (g + 1) * gw)
        bg = conv_ref[:, D_INNER + g * D_STATE:D_INNER + (g + 1) * D_STATE].astype(BF16)
        cg = conv_ref[:, D_INNER + SSD_GROUPS * D_STATE + g * D_STATE:
                      D_INNER + SSD_GROUPS * D_STATE + (g + 1) * D_STATE].astype(BF16)
        cbm = _dot_nt(cg, bg)
        xd_g = xdt_ref[:, sl]
        m_parts, bd_parts = [], []
        for r in range(SSD_HPG):
            h = g * SSD_HPG + r
            seg = a_l2[:, h:h + 1] - a_l2_t[h:h + 1, 0:rows]
            decay = jnp.exp2(jnp.where(mask, seg, -jnp.inf))
            m_parts.append((cbm * decay).astype(BF16))
            bd_parts.append(jnp.where(head_of_lane == r, xd_g, jnp.zeros_like(xd_g)))
        y = _dot(jnp.concatenate(m_parts, axis=1), jnp.concatenate(bd_parts, axis=0))
        yo_parts = []
        for j in range(nseg):
            r0 = j * seg_len
            s_old = st_ref[j, g * SSD_HPG:(g + 1) * SSD_HPG].reshape(gw, D_STATE)
            yo_parts.append(_dot_nt(cg[r0:r0 + seg_len], s_old.astype(BF16)))
            dec = jnp.concatenate(
                [jnp.broadcast_to(cdec[r0:r0 + 1, g * SSD_HPG + r:g * SSD_HPG + r + 1], (SSD_HEAD_DIM, D_STATE))
                 for r in range(SSD_HPG)], axis=0)
            s_new = s_old * dec + _dot_tn(w_ref[r0:r0 + seg_len, sl], bg[r0:r0 + seg_len])
            st_ref[j, g * SSD_HPG:(g + 1) * SSD_HPG] = s_new.reshape(SSD_HPG, SSD_HEAD_DIM, D_STATE)
        yo = yo_parts[0] if nseg == 1 else jnp.concatenate(yo_parts, axis=0)
        y = y + yo * e2_ref[:, sl]
        y = y + dskip_ref[:, sl] * conv_ref[:, sl]
        y = y * _silu(z_ref[:, sl])
        y = y * lax.rsqrt(jnp.mean(y * y, axis=-1, keepdims=True) + NORM_EPS) * norm_ref[:, sl]
        y_ref[:, sl] = y.astype(BF16)


def _ssd(proj, l, prm, mats, st_shape, st_prev, n_outer, n_inner, rows, seg_len, token_lo, token_hi,
         conv_state=None, ssm_state=None):
    has_state = ssm_state is not None
    nseg = rows // seg_len
    n = proj.shape[0]
    st_block = (None, nseg, SSD_HEADS, SSD_HEAD_DIM, D_STATE)

    def rowmap(cb):
        return lambda o, c: (o * n_inner + c, cb)

    def st_map(o, c):
        return (l, o, 0, 0, 0)

    in_specs = [pl.BlockSpec((rows, CONV_DIM), rowmap(COL_XBC // CONV_DIM)),
                pl.BlockSpec((rows, D_INNER), rowmap(COL_Z // D_INNER)),
                pl.BlockSpec((rows, LANES), rowmap(COL_DT // LANES)),
                _layer_spec(l, (CONV_WIDTH, CONV_DIM)), _layer_spec(l, (1, CONV_DIM)),
                _layer_spec(l, (1, LANES)), _layer_spec(l, (1, LANES)),
                _layer_spec(l, (1, D_INNER)), _layer_spec(l, (1, D_INNER)),
                _const_spec((rows, rows)), _const_spec((rows, rows)), _const_spec((rows, rows)),
                _const_spec((3 * LANES, D_INNER))]
    args = [proj, proj, proj, prm["conv_w"], prm["conv_b"], prm["dt_bias"], prm["a_log"],
            prm["d_skip"], prm["ssd_norm"], mats[0], mats[1], mats[2], mats[3]]
    if has_state:
        in_specs += [pl.BlockSpec((None, CONV_WIDTH - 1, rows, CONV_DIM), lambda o, c: (l, 0, o * n_inner + c, 0)),
                     pl.BlockSpec(st_block, st_map)]
        args += [conv_state, ssm_state]
    aliases = {}
    if st_prev is not None:
        in_specs.append(_ANY)
        args.append(st_prev)
        aliases = {len(args) - 1: 1}
    y, st = pl.pallas_call(
        functools.partial(_ssd_kernel, rows=rows, seg_len=seg_len, has_state=has_state,
                          has_prev=st_prev is not None, token_lo=token_lo, token_hi=token_hi),
        grid=(n_outer, n_inner),
        in_specs=in_specs,
        out_specs=[pl.BlockSpec((rows, D_INNER), rowmap(0)), pl.BlockSpec(st_block, st_map)],
        out_shape=[jax.ShapeDtypeStruct((n, D_INNER), BF16),
                   jax.ShapeDtypeStruct(st_shape, F32)],
        scratch_shapes=[pltpu.VMEM((CONV_DIM // LANES, 16 + 2 * rows, LANES), F32),
                        pltpu.VMEM((rows, CONV_DIM), F32),
                        pltpu.VMEM((rows, D_INNER), F32),
                        pltpu.VMEM((rows, D_INNER), BF16),
                        pltpu.VMEM((rows, D_INNER), BF16)],
        input_output_aliases=aliases,
        compiler_params=_cparams(("parallel", "arbitrary")),
        name="ssd_state" if has_state else "ssd_prompt",
    )(*args)
    return y, st


def _attn_prompt_kernel(q0, q1, k0, k1, v0, v1, bias_ref, *rest, dil, seq, keep):
    o_ref, lse_ref, kvt_ref = rest[-3:]
    nblk = seq // (ATTN_BLOCK * dil)
    lo_lanes = lax.broadcasted_iota(jnp.int32, (ATTN_BLOCK, LANES), 1) < ATTN_HEAD_DIM

    tc = min(keep, 512)
    for half, (k, v) in enumerate(((k0, v0), (k1, v1))):
        for c0 in range(0, keep, tc):
            src = pl.ds(seq - keep + c0, tc)
            kvt_ref[half * LANES:(half + 1) * LANES, c0:c0 + tc] = k[src, :].T
            kvt_ref[ATTN_GROUP_WIDTH + half * LANES:ATTN_GROUP_WIDTH + (half + 1) * LANES, c0:c0 + tc] = v[src, :].T

    def rows(start):
        if dil == 1:
            return pl.ds(pl.multiple_of(start, ATTN_BLOCK), ATTN_BLOCK)
        return pl.ds(start, ATTN_BLOCK, stride=dil)

    def body(i, carry):
        r = i // nblk
        blk = i % nblk
        cur = rows(r + blk * ATTN_BLOCK * dil)
        prev = rows(r + jnp.maximum(blk - 1, 0) * ATTN_BLOCK * dil)
        tab = jnp.minimum(blk, 1)
        for half, (q, k, v) in enumerate(((q0, k0, v0), (q1, k1, v1))):
            qb = q[cur, :] * ATTN_SCALE
            kk = jnp.concatenate([k[prev, :], k[cur, :]], axis=0).astype(BF16)
            vv = jnp.concatenate([v[prev, :], v[cur, :]], axis=0).astype(BF16)
            outs, lses = [], []
            for hh in range(2):
                lanes = lo_lanes if hh == 0 else jnp.logical_not(lo_lanes)
                qm = jnp.where(lanes, qb, 0.0).astype(BF16)
                s = _dot_nt(qm, kk) + bias_ref[tab, 2 * half + hh]
                m = jnp.max(s, axis=-1, keepdims=True)
                p = jnp.exp(s - m)
                l = jnp.sum(p, axis=-1, keepdims=True)
                outs.append(_dot(p.astype(BF16), vv) / l)
                lses.append(jnp.broadcast_to(m + jnp.log(l), (ATTN_BLOCK, LANES)))
            o_ref[half, cur, :] = jnp.where(lo_lanes, outs[0], outs[1])
            lse_ref[half, cur, :] = jnp.where(lo_lanes, lses[0], lses[1])
        return carry

    lax.fori_loop(0, dil * nblk, body, 0, unroll=4)


def _attn_prompt(proj, l, bias, kvt_shape, kvt_prev, gi, batch, seq):
    dil = DILATIONS[gi]
    keep = kvt_shape[-1]
    base = (COL_QKV + gi * ATTN_WIDTH) // LANES

    def col(j):
        return pl.BlockSpec((seq, LANES), lambda b: (b, base + j))

    out_spec = pl.BlockSpec((2, seq, LANES), lambda b: (0, b, 0))
    shape = jax.ShapeDtypeStruct((2, batch * seq, LANES), F32)
    prev = [] if kvt_prev is None else [kvt_prev]
    return pl.pallas_call(
        functools.partial(_attn_prompt_kernel, dil=dil, seq=seq, keep=keep),
        grid=(batch,),
        in_specs=[col(0), col(1), col(2), col(3), col(4), col(5),
                  _const_spec((2, HEADS_PER_GROUP, ATTN_BLOCK, 2 * ATTN_BLOCK))] + ([_ANY] if prev else []),
        out_specs=[out_spec, out_spec,
                   pl.BlockSpec((None, None, KV_WIDTH, keep), lambda b: (l, b, 0, 0))],
        out_shape=[shape, shape, jax.ShapeDtypeStruct(kvt_shape, F32)],
        input_output_aliases={7: 2} if prev else {},
        compiler_params=_cparams(("parallel",)),
        name=f"attn_prompt_d{dil}",
    )(proj, proj, proj, proj, proj, proj, bias, *prev)


def _attn_sample_kernel(*refs, bt):
    n_in = 9 * N_DIL_GROUPS
    ins, outs = refs[:n_in], refs[n_in:]
    lo8 = lax.broadcasted_iota(jnp.int32, (SEG, LANES), 1) < ATTN_HEAD_DIM
    lo16 = lax.broadcasted_iota(jnp.int32, (2 * SEG, LANES), 1) < ATTN_HEAD_DIM
    top16 = lax.broadcasted_iota(jnp.int32, (2 * SEG, LANES), 0) < SEG
    head_lanes = lo16 == top16

    for b in range(bt):
        seg = pl.ds(b * SEG, SEG)
        for gi in range(N_DIL_GROUPS):
            q0, q1, kn0, kn1, vn0, vn1, cache_ref, bias_c_ref, bias_n_ref = ins[9 * gi:9 * gi + 9]
            o_ref, lse_ref = outs[2 * gi:2 * gi + 2]
            for half, (q, kn, vn) in enumerate(((q0, kn0, vn0), (q1, kn1, vn1))):
                qb = q[seg, :] * ATTN_SCALE
                q16 = jnp.where(head_lanes, jnp.concatenate([qb, qb], axis=0), 0.0).astype(BF16)
                knew = kn[seg, :].astype(BF16)
                vnew = vn[seg, :].astype(BF16)
                kt = cache_ref[b, half * LANES:(half + 1) * LANES, :].astype(BF16)
                vt = cache_ref[b, ATTN_GROUP_WIDTH + half * LANES:
                               ATTN_GROUP_WIDTH + (half + 1) * LANES, :].astype(BF16)
                sc = _dot(q16, kt) + bias_c_ref[half]
                sn = _dot_nt(q16, knew) + bias_n_ref[half]
                m = jnp.maximum(jnp.max(sc, axis=-1, keepdims=True), jnp.max(sn, axis=-1, keepdims=True))
                pc = jnp.exp(sc - m)
                pn = jnp.exp(sn - m)
                l = jnp.sum(pc, axis=-1, keepdims=True) + jnp.sum(pn, axis=-1, keepdims=True)
                o = (_dot_nt(pc.astype(BF16), vt) + _dot(pn.astype(BF16), vnew)) / l
                lse = jnp.broadcast_to(m + jnp.log(l), (2 * SEG, LANES))
                o_ref[half, seg, :] = jnp.where(lo8, o[0:SEG], o[SEG:2 * SEG])
                lse_ref[half, seg, :] = jnp.where(lo8, lse[0:SEG], lse[SEG:2 * SEG])


def _attn_sample(proj, l, caches_t, biases, bt):
    nb = caches_t[0].shape[1]
    in_specs, args = [], []
    for gi in range(N_DIL_GROUPS):
        rows_c = caches_t[gi].shape[3]
        base = (COL_QKV + gi * ATTN_WIDTH) // LANES
        in_specs += [pl.BlockSpec((bt * SEG, LANES), lambda i, c=base + j: (i, c)) for j in range(6)]
        in_specs += [pl.BlockSpec((None, bt, KV_WIDTH, rows_c), lambda i: (l, i, 0, 0)),
                     _const_spec((2, 2 * SEG, rows_c)), _const_spec((2, 2 * SEG, SEG))]
        args += [proj] * 6 + [caches_t[gi], biases[gi][0], biases[gi][1]]
    out_spec = pl.BlockSpec((2, bt * SEG, LANES), lambda i: (0, i, 0))
    shape = jax.ShapeDtypeStruct((2, nb * SEG, LANES), F32)
    outs = pl.pallas_call(
        functools.partial(_attn_sample_kernel, bt=bt),
        grid=(nb // bt,),
        in_specs=in_specs,
        out_specs=[out_spec] * (2 * N_DIL_GROUPS),
        out_shape=[shape] * (2 * N_DIL_GROUPS),
        compiler_params=_cparams(("parallel",)),
        name="attn_sample",
    )(*args)
    return [(outs[2 * gi], outs[2 * gi + 1]) for gi in range(N_DIL_GROUPS)]


def _mix_out_kernel(x_ref, y_ref, gs_ref, ga_ref, o0, o1, o2, l0, l1, l2, wso_ref, wao_ref, wo_ref, out_ref,
                    *, token_lo, token_hi, seg_len):
    ssd_out = _dot(y_ref[...], wso_ref[...])
    halves = []
    for half in range(2):
        ls = [l0[half], l1[half], l2[half]]
        os_ = [o0[half], o1[half], o2[half]]
        m = jnp.maximum(jnp.maximum(ls[0], ls[1]), ls[2])
        es = [jnp.exp(l - m) for l in ls]
        den = es[0] + es[1] + es[2]
        o = (es[0] / den) * os_[0] + (es[1] / den) * os_[1] + (es[2] / den) * os_[2]
        halves.append(o.astype(BF16))
    attn_out = _dot(jnp.concatenate(halves, axis=1), wao_ref[...])
    merged = jax.nn.sigmoid(gs_ref[...]) * ssd_out + jax.nn.sigmoid(ga_ref[...]) * attn_out
    upd = _dot(merged.astype(BF16), wo_ref[...])
    if seg_len is not None:
        rows = x_ref.shape[0]
        ris = lax.broadcasted_iota(jnp.int32, (rows, 1), 0) % seg_len
        upd = jnp.where((ris >= token_lo) & (ris < token_hi), upd, 0.0)
    out_ref[...] = x_ref[...] + upd


def _mix_out(x, y, proj, attn, l, prm, tm, seg_len=None):
    n = x.shape[0]
    half_spec = pl.BlockSpec((2, tm, LANES), lambda i: (0, i, 0))
    (o0, l0), (o1, l1), (o2, l2) = attn
    return pl.pallas_call(
        functools.partial(_mix_out_kernel, token_lo=SEG_T0, token_hi=SEG_T0 + 4, seg_len=seg_len),
        grid=(n // tm,),
        in_specs=[pl.BlockSpec((tm, D_MODEL), lambda i: (i, 0)),
                  pl.BlockSpec((tm, D_INNER), lambda i: (i, 0)),
                  pl.BlockSpec((tm, D_MODEL), lambda i: (i, COL_GS // D_MODEL)),
                  pl.BlockSpec((tm, D_MODEL), lambda i: (i, COL_GA // D_MODEL)),
                  half_spec, half_spec, half_spec, half_spec, half_spec, half_spec,
                  _layer_spec(l, (D_INNER, D_MODEL), single=True),
                  _layer_spec(l, (ATTN_GROUP_WIDTH, D_MODEL), single=True),
                  _layer_spec(l, (D_MODEL, D_MODEL), single=True)],
        out_specs=pl.BlockSpec((tm, D_MODEL), lambda i: (i, 0)),
        out_shape=jax.ShapeDtypeStruct((n, D_MODEL), F32),
        compiler_params=_cparams(("parallel",)),
        name="mix_out",
    )(x, y, proj, proj, o0, o1, o2, l0, l1, l2, prm["w_ssd_out"], prm["w_attn_out"], prm["w_o"])


def _rel_bucket(dist):
    max_exact = REL_BUCKETS // 2
    d = jnp.maximum(dist, 1).astype(F32)
    large = max_exact + (jnp.log(d / max_exact) / math.log(REL_MAX_DIST / max_exact)
                         * (REL_BUCKETS - max_exact)).astype(jnp.int32)
    large = jnp.minimum(large, REL_BUCKETS - 1)
    return jnp.where(dist < max_exact, dist, large)


def _bias_lookup(tab, dist):
    onehot = (_rel_bucket(dist)[..., None] == jnp.arange(REL_BUCKETS)).astype(F32)
    return jnp.einsum("...k,kh->h...", onehot, tab.astype(F32), precision=lax.Precision.HIGHEST)


def _prompt_bias(rel_bias, gi):
    dil, reach = DILATIONS[gi], WINDOWS[gi] // DILATIONS[gi]
    blk = ATTN_BLOCK
    step = (jnp.arange(blk)[:, None] + blk) - jnp.arange(2 * blk)[None, :]
    tab = rel_bias[:, gi * HEADS_PER_GROUP:(gi + 1) * HEADS_PER_GROUP]
    bias = _bias_lookup(tab, jnp.maximum(step, 0) * dil)
    valid = (step >= 0) & (step <= reach)
    first = valid & (jnp.arange(2 * blk)[None, :] >= blk)
    neg = jnp.float32(-jnp.inf)
    return jnp.stack([jnp.where(first[None], bias, neg), jnp.where(valid[None], bias, neg)])


def _sample_bias(rel_bias, gi, row_pos):
    dil, window = DILATIONS[gi], WINDOWS[gi]
    tab = rel_bias[:, gi * HEADS_PER_GROUP:(gi + 1) * HEADS_PER_GROUP]
    seg_row = jnp.arange(SEG)
    is_tok = (seg_row >= SEG_T0) & (seg_row < SEG_T0 + 4)
    t = jnp.where(is_tok, seg_row - SEG_T0, 0)
    neg = jnp.float32(-jnp.inf)

    def table(dist, ok):
        ok = ok & (dist >= 0) & (dist % dil == 0) & (dist // dil <= window // dil)
        tbl = jnp.where(ok[None], _bias_lookup(tab, jnp.maximum(dist, 0)), neg)
        return tbl.reshape(2, 2 * SEG, dist.shape[1])

    dist_c = t[:, None] - row_pos[None, :]
    bias_c = table(dist_c, jnp.ones(dist_c.shape, bool))
    key_tok = seg_row - SEG_T0
    dist_n = t[:, None] - key_tok[None, :]
    bias_n = table(dist_n, jnp.broadcast_to(is_tok[None, :], dist_n.shape))
    return bias_c, bias_n


def _ssd_mats(rows, seg_len):
    l = jnp.arange(rows)[:, None]
    s = jnp.arange(rows)[None, :]
    same = (l // seg_len) == (s // seg_len)
    causal = same & (s <= l)
    head = jnp.arange(LANES)[:, None]
    chan = jnp.arange(D_INNER)[None, :] // SSD_HEAD_DIM
    spread = jnp.tile((head == chan).astype(BF16), (3, 1))
    return causal.astype(F32), same.astype(F32), causal.astype(F32), spread


def _kv_rows_minor(c):
    d, b, rows = c.shape[:3]
    return c.transpose(0, 1, 3, 4, 5, 2).reshape(d, b, KV_WIDTH, rows)


def kernel(x_prompt, x_sample, cache_kv_w128, cache_kv_w512, cache_kv_w2048, state_conv, state_ssm, rel_bias,
           ffn1_norm, ffn1_w13, ffn1_w2, mix_norm, w_in, conv_w, conv_b, dt_bias, a_log, d_skip, ssd_norm,
           w_ssd_out, w_attn_out, w_o, ffn2_norm, ffn2_w13, ffn2_w2, final_norm):
    batch, seq, _ = x_prompt.shape
    dec_batch, dec_seq, _ = x_sample.shape
    n_p = batch * seq
    n_s = dec_batch * SEG
    tm_p, tm_s = 512, min(512, n_s)
    tmi_p, tmi_s = 2048, min(1024, n_s)
    tok_lo, tok_hi = SEG_T0, SEG_T0 + dec_seq

    w_t = jnp.swapaxes(w_in, 1, 2)
    z_w, xbc_w, dt_w, q_w, k_w, v_w, gs_w, ga_w = jnp.split(
        w_t, [2048, 6144, 6176, 6944, 7712, 8480, 9504], axis=1)
    qkv_w = []
    for gi in range(N_DIL_GROUPS):
        sl = slice(gi * ATTN_GROUP_WIDTH, (gi + 1) * ATTN_GROUP_WIDTH)
        qkv_w += [q_w[:, sl], k_w[:, sl], v_w[:, sl]]
    pad_w = jnp.zeros((DEPTH, PROJ_W - COL_DT - SSD_HEADS, D_MODEL), w_in.dtype)
    w_in_t = jnp.concatenate([xbc_w, z_w, gs_w, ga_w] + qkv_w + [dt_w, pad_w], axis=1).astype(BF16)

    def pad_heads(v):
        return jnp.pad(v, ((0, 0), (0, LANES - SSD_HEADS)))[:, None, :]

    prm = dict(
        ffn1_norm=ffn1_norm[:, None], ffn1_w13=ffn1_w13.astype(BF16), ffn1_w2=ffn1_w2.astype(BF16),
        mix_norm=mix_norm[:, None], conv_w=conv_w, conv_b=conv_b[:, None],
        dt_bias=pad_heads(dt_bias), a_log=pad_heads(a_log),
        d_skip=jnp.repeat(d_skip, SSD_HEAD_DIM, axis=1)[:, None], ssd_norm=ssd_norm[:, None],
        w_ssd_out=w_ssd_out.astype(BF16), w_attn_out=w_attn_out.astype(BF16), w_o=w_o.astype(BF16),
        ffn2_norm=ffn2_norm[:, None], ffn2_w13=ffn2_w13.astype(BF16), ffn2_w2=ffn2_w2.astype(BF16))

    mats_p = _ssd_mats(SSD_CHUNK, SSD_CHUNK)
    rows_s = 8 * SEG
    mats_s = _ssd_mats(rows_s, SEG)
    bias_p = [_prompt_bias(rel_bias, gi) for gi in range(N_DIL_GROUPS)]
    caches = (cache_kv_w128, cache_kv_w512, cache_kv_w2048)
    bias_s, caches_t = [], []
    for gi in range(N_DIL_GROUPS):
        lb = caches[gi].shape[2]
        bias_s.append(_sample_bias(rel_bias, gi, jnp.arange(lb) - lb))
        caches_t.append(_kv_rows_minor(caches[gi]))
    conv_state_rows = jnp.stack(
        [jnp.pad(state_conv[:, :, CONV_WIDTH - 1 - k:, :], ((0, 0), (0, 0), (0, SEG - k), (0, 0))).reshape(
            DEPTH, n_s, CONV_DIM) for k in range(1, CONV_WIDTH)], axis=1)
    sample_bt = 4

    xp = x_prompt.reshape(n_p, D_MODEL)
    xs = jnp.pad(x_sample, ((0, 0), (SEG_T0, SEG - SEG_T0 - dec_seq), (0, 0))).reshape(n_s, D_MODEL)
    ssm_p_shape = (DEPTH, batch, SSD_HEADS, SSD_HEAD_DIM, D_STATE)
    ssm_s_shape = (DEPTH, dec_batch, SSD_HEADS, SSD_HEAD_DIM, D_STATE)
    kvt_shapes = [(DEPTH, batch, KV_WIDTH, min(WINDOWS[gi], seq)) for gi in range(N_DIL_GROUPS)]
    ssm_p = ssm_s = None
    kvt_p = [None] * N_DIL_GROUPS
    fn = final_norm[None]

    kv_s = [[], [], []]
    conv_p, conv_s = [], []
    for l in range(DEPTH):
        xp = _ffn(xp, l, prm["ffn1_norm"], prm["ffn1_w13"], prm["ffn1_w2"], tm_p)
        xs = _ffn(xs, l, prm["ffn1_norm"], prm["ffn1_w13"], prm["ffn1_w2"], tm_s)
        proj_p = _inproj(xp, l, prm["mix_norm"], w_in_t, tmi_p)
        proj_s = _inproj(xs, l, prm["mix_norm"], w_in_t, tmi_s)

        y_p, ssm_p = _ssd(proj_p, l, prm, mats_p, ssm_p_shape, ssm_p, batch, seq // SSD_CHUNK, SSD_CHUNK,
                          SSD_CHUNK, 0, SSD_CHUNK)
        y_s, ssm_s = _ssd(proj_s, l, prm, mats_s, ssm_s_shape, ssm_s, n_s // rows_s, 1, rows_s, SEG,
                          tok_lo, tok_hi, conv_state=conv_state_rows, ssm_state=state_ssm)
        attn_p = []
        for gi in range(N_DIL_GROUPS):
            o, lse, kvt_p[gi] = _attn_prompt(proj_p, l, bias_p[gi], kvt_shapes[gi], kvt_p[gi], gi, batch, seq)
            attn_p.append((o, lse))
        attn_s = _attn_sample(proj_s, l, caches_t, bias_s, sample_bt)

        xp = _mix_out(xp, y_p, proj_p, attn_p, l, prm, tm_p)
        xs = _mix_out(xs, y_s, proj_s, attn_s, l, prm, tm_s, seg_len=SEG)
        last = fn if l == DEPTH - 1 else None
        xp = _ffn(xp, l, prm["ffn2_norm"], prm["ffn2_w13"], prm["ffn2_w2"], tm_p, final_gain=last)
        xs = _ffn(xs, l, prm["ffn2_norm"], prm["ffn2_w13"], prm["ffn2_w2"], tm_s, final_gain=last)

        pp = proj_p.reshape(batch, seq, PROJ_W)
        conv_p.append(pp[:, seq - (CONV_WIDTH - 1):, COL_XBC:COL_XBC + CONV_DIM])
        conv_s.append(proj_s[:, COL_XBC:COL_XBC + CONV_DIM].reshape(dec_batch, SEG, CONV_DIM)[
            :, tok_hi - (CONV_WIDTH - 1):tok_hi])
        for gi in range(N_DIL_GROUPS):
            c0 = COL_QKV + gi * ATTN_WIDTH + ATTN_GROUP_WIDTH
            kv_s[gi].append(proj_s[:, c0:c0 + KV_WIDTH].reshape(
                dec_batch, SEG, 2, HEADS_PER_GROUP, ATTN_HEAD_DIM)[:, tok_lo:tok_hi])

    def kv_prompt(kvt):
        keep = kvt.shape[-1]
        return kvt.reshape(DEPTH, batch, 2, HEADS_PER_GROUP, ATTN_HEAD_DIM, keep).transpose(0, 1, 5, 2, 3, 4)

    y_prompt = xp.reshape(batch, seq, D_MODEL)
    y_sample = xs.reshape(dec_batch, SEG, D_MODEL)[:, tok_lo:tok_hi]
    return (y_prompt, y_sample,
            kv_prompt(kvt_p[0]), kv_prompt(kvt_p[1]), kv_prompt(kvt_p[2]), jnp.stack(conv_p), ssm_p,
            jnp.stack(kv_s[0]), jnp.stack(kv_s[1]), jnp.stack(kv_s[2]), jnp.stack(conv_s), ssm_s)
```
